```python
import jax, jax.numpy as jnp
from jax import lax
import numpy as np

D_MODEL = 4096
BATCH = 8
SEQ = 2048
DEPTH = 2
DEC_BATCH = 4
DEC_SEQ = 4096
PAST_LEN = 128

CHUNK = 128
D_A = D_MODEL // 2
A_HEAD_DIM = 128
A_HEADS = D_A // A_HEAD_DIM
D_B = D_MODEL // 2
B_GROUPS = 8
B_GROUP_DIM = D_B // B_GROUPS
D_IN = 3 * D_A + 2 * D_B + 2 * D_MODEL
SPLITS = (D_A, 2 * D_A, 3 * D_A, 3 * D_A + D_B, 3 * D_A + 2 * D_B, 3 * D_A + 2 * D_B + D_MODEL)
EPS = 1e-6

kernel_name = "gated_gmlp_fnet_hybrid_encoder"


def rms_norm(x, g):
    xf = x.astype(jnp.float32)
    y = xf * lax.rsqrt(jnp.mean(xf * xf, axis=-1, keepdims=True) + EPS)
    return (y * g.astype(jnp.float32)).astype(x.dtype)


def layer_norm(x, g, b):
    xf = x.astype(jnp.float32)
    mu = jnp.mean(xf, axis=-1, keepdims=True)
    xc = xf - mu
    var = jnp.mean(xc * xc, axis=-1, keepdims=True)
    y = xc * lax.rsqrt(var + EPS)
    return (y * g.astype(jnp.float32) + b.astype(jnp.float32)).astype(x.dtype)


def spatial_gating_branch(u, v, z, ln_g, ln_b, w_s, b_s):
    bsz, s, _ = u.shape
    u = jax.nn.gelu(u)
    v = layer_norm(jax.nn.gelu(v), ln_g, ln_b)
    vc = v.reshape(bsz, s // CHUNK, CHUNK, A_HEADS, A_HEAD_DIM)
    mixed = jnp.einsum('hpq,bcqhd->bcphd', w_s, vc) + jnp.transpose(b_s)[None, None, :, :, None]
    mixed = mixed.reshape(bsz, s, D_A)
    return u * mixed * jax.nn.silu(z)


def fourier_branch(xb, z):
    bsz, s, _ = xb.shape
    xg = xb.astype(jnp.float32).reshape(bsz, s, B_GROUPS, B_GROUP_DIM)
    f = jnp.fft.fft2(xg, axes=(1, 3), norm='ortho').real
    f = f.reshape(bsz, s, D_B).astype(xb.dtype)
    return f * jax.nn.silu(z)


def mixer_layer(x, norm_g, w_in, sgu_ln_g, sgu_ln_b, w_spatial, b_spatial, w_a, w_b, b_gate, w_out):
    h = rms_norm(x, norm_g)
    p = jnp.einsum('bsd,de->bse', h, w_in)
    u, v, z_a, xb, z_b, g_a, g_b = jnp.split(p, SPLITS, axis=-1)
    y_a = jnp.einsum('bsc,cd->bsd', spatial_gating_branch(u, v, z_a, sgu_ln_g, sgu_ln_b, w_spatial, b_spatial), w_a)
    y_b = jnp.einsum('bsc,cd->bsd', fourier_branch(xb, z_b), w_b)
    m = jax.nn.sigmoid(g_a + b_gate[0]) * y_a + jax.nn.sigmoid(g_b + b_gate[1]) * y_b
    return x + jnp.einsum('bsd,de->bse', m, w_out)


def trunk(x, norm_g, w_in, sgu_ln_g, sgu_ln_b, w_spatial, b_spatial, w_a, w_b, b_gate, w_out, final_g):
    for l in range(DEPTH):
        x = mixer_layer(x, norm_g[l], w_in[l], sgu_ln_g[l], sgu_ln_b[l], w_spatial[l], b_spatial[l],
                        w_a[l], w_b[l], b_gate[l], w_out[l])
    return rms_norm(x, final_g)


def setup_inputs(seed: int = 0) -> dict:
    key = jax.random.key(seed)
    ks = jax.random.split(key, 14)
    f32 = jnp.float32
    x_prompt = jax.random.normal(ks[0], (BATCH, SEQ, D_MODEL), f32)
    x_sample = jax.random.normal(ks[1], (DEC_BATCH, DEC_SEQ, D_MODEL), f32)
    norm_g = 1.0 + 0.02 * jax.random.normal(ks[2], (DEPTH, D_MODEL), f32)
    w_in = jax.random.normal(ks[3], (DEPTH, D_MODEL, D_IN), f32) * D_MODEL ** -0.5
    sgu_ln_g = 1.0 + 0.02 * jax.random.normal(ks[4], (DEPTH, D_A), f32)
    sgu_ln_b = 0.02 * jax.random.normal(ks[5], (DEPTH, D_A), f32)
    w_spatial = jax.random.normal(ks[6], (DEPTH, A_HEADS, CHUNK, CHUNK), f32) * CHUNK ** -0.5
    b_spatial = 1.0 + 0.02 * jax.random.normal(ks[7], (DEPTH, A_HEADS, CHUNK), f32)
    w_a = jax.random.normal(ks[8], (DEPTH, D_A, D_MODEL), f32) * D_A ** -0.5
    w_b = jax.random.normal(ks[9], (DEPTH, D_B, D_MODEL), f32) * D_B ** -0.5
    b_gate = 0.02 * jax.random.normal(ks[10], (DEPTH, 2, D_MODEL), f32)
    w_out = jax.random.normal(ks[11], (DEPTH, D_MODEL, D_MODEL), f32) * D_MODEL ** -0.5
    final_g = 1.0 + 0.02 * jax.random.normal(ks[12], (D_MODEL,), f32)
    return {"x_prompt": x_prompt, "x_sample": x_sample, "norm_g": norm_g, "w_in": w_in,
            "sgu_ln_g": sgu_ln_g, "sgu_ln_b": sgu_ln_b, "w_spatial": w_spatial, "b_spatial": b_spatial,
            "w_a": w_a, "w_b": w_b, "b_gate": b_gate, "w_out": w_out, "final_g": final_g}


def reference(x_prompt, x_sample, norm_g, w_in, sgu_ln_g, sgu_ln_b, w_spatial, b_spatial, w_a, w_b, b_gate, w_out, final_g):
    y_prompt = trunk(x_prompt, norm_g, w_in, sgu_ln_g, sgu_ln_b, w_spatial, b_spatial, w_a, w_b, b_gate, w_out, final_g)
    y_sample = trunk(x_sample, norm_g, w_in, sgu_ln_g, sgu_ln_b, w_spatial, b_spatial, w_a, w_b, b_gate, w_out, final_g)
    return (y_prompt, y_sample)
```

```python
import functools

import numpy as np
import jax
import jax.numpy as jnp
from jax import lax
from jax.experimental import pallas as pl
from jax.experimental.pallas import tpu as pltpu

B_GROUPS = 8
EPS = 1e-6
VMEM_LIMIT_BYTES = 56 * 1024 * 1024
ROW_TILE = 1024
COL_TILE = 512
F32 = jnp.float32
BF16 = jnp.bfloat16


def _params(*sem):
    return pltpu.CompilerParams(dimension_semantics=sem, vmem_limit_bytes=VMEM_LIMIT_BYTES)


def _dot(a, b):
    return jnp.dot(a, b, preferred_element_type=F32)


def _rmsnorm_kernel(x_ref, g_ref, o_ref):
    x = x_ref[...]
    ms = jnp.mean(x * x, axis=-1, keepdims=True)
    o_ref[...] = (x * lax.rsqrt(ms + EPS) * g_ref[...]).astype(o_ref.dtype)


def _rmsnorm(x, g, out_dtype, bm):
    t, d = x.shape
    return pl.pallas_call(
        _rmsnorm_kernel,
        grid=(t // bm,),
        in_specs=[pl.BlockSpec((bm, d), lambda i: (i, 0)),
                  pl.BlockSpec((1, d), lambda i: (0, 0))],
        out_specs=pl.BlockSpec((bm, d), lambda i: (i, 0)),
        out_shape=jax.ShapeDtypeStruct((t, d), out_dtype),
        compiler_params=_params("parallel"),
        name="rmsnorm",
    )(x, g.reshape(1, d))


def _spatial_kernel(h_ref, w_ref, lng_ref, lnb_ref, ws_ref, bs_ref, o_ref, gv_ref, *,
                    n_tiles, bn, chunk, hd, d_a):
    j = pl.program_id(1)
    gv_ref[j] = jax.nn.gelu(_dot(h_ref[...], w_ref[...]))

    @pl.when(j == n_tiles - 1)
    def _():
        n_chunks = gv_ref.shape[1] // chunk
        heads_per_tile = bn // hd

        def body(c, carry):
            rows = pl.ds(pl.multiple_of(c * chunk, chunk), chunk)
            tiles = [gv_ref[t, rows, :] for t in range(n_tiles)]
            mu = sum(jnp.sum(x, axis=-1, keepdims=True) for x in tiles) / d_a
            cent = [x - mu for x in tiles]
            var = sum(jnp.sum(x * x, axis=-1, keepdims=True) for x in cent) / d_a
            rstd = lax.rsqrt(var + EPS)
            for t in range(n_tiles):
                cols = slice(t * bn, (t + 1) * bn)
                vn = (cent[t] * rstd * lng_ref[:, cols] + lnb_ref[:, cols]).astype(BF16)
                for hh in range(heads_per_tile):
                    head = t * heads_per_tile + hh
                    mixed = _dot(ws_ref[head], vn[:, hh * hd:(hh + 1) * hd]) + bs_ref[head]
                    o_ref[rows, head * hd:(head + 1) * hd] = mixed.astype(o_ref.dtype)
            return carry

        lax.fori_loop(0, n_chunks, body, 0)


def _spatial(h, w_in, col0, ln_g, ln_b, w_s, b_s_full, bm, bn):
    t, d = h.shape
    heads, chunk, _ = w_s.shape
    d_a = ln_g.shape[0]
    hd = d_a // heads
    n_tiles = d_a // bn
    blk0 = col0 // bn
    kern = functools.partial(_spatial_kernel, n_tiles=n_tiles, bn=bn, chunk=chunk, hd=hd, d_a=d_a)
    return pl.pallas_call(
        kern,
        grid=(t // bm, n_tiles),
        in_specs=[pl.BlockSpec((bm, d), lambda i, j: (i, 0)),
                  pl.BlockSpec((d, bn), lambda i, j: (0, blk0 + j)),
                  pl.BlockSpec((1, d_a), lambda i, j: (0, 0)),
                  pl.BlockSpec((1, d_a), lambda i, j: (0, 0)),
                  pl.BlockSpec((heads, chunk, chunk), lambda i, j: (0, 0, 0)),
                  pl.BlockSpec((heads, chunk, hd), lambda i, j: (0, 0, 0))],
        out_specs=pl.BlockSpec((bm, d_a), lambda i, j: (i, 0)),
        out_shape=jax.ShapeDtypeStruct((t, d_a), BF16),
        scratch_shapes=[pltpu.VMEM((n_tiles, bm, bn), F32)],
        compiler_params=_params("parallel", "arbitrary"),
        name="inproj_spatial",
    )(h, w_in, ln_g.reshape(1, d_a), ln_b.reshape(1, d_a), w_s, b_s_full)


def _gate_a_kernel(h_ref, wu_ref, wz_ref, mixed_ref, o_ref):
    h = h_ref[...]
    u = jax.nn.gelu(_dot(h, wu_ref[...]))
    z = jax.nn.silu(_dot(h, wz_ref[...]))
    o_ref[...] = (u * mixed_ref[...].astype(F32) * z).astype(o_ref.dtype)


def _gate_a(h, w_in, col_u, col_z, mixed, bm, bn):
    t, d = h.shape
    d_a = mixed.shape[1]
    bu, bz = col_u // bn, col_z // bn
    return pl.pallas_call(
        _gate_a_kernel,
        grid=(t // bm, d_a // bn),
        in_specs=[pl.BlockSpec((bm, d), lambda i, j: (i, 0)),
                  pl.BlockSpec((d, bn), lambda i, j: (0, bu + j)),
                  pl.BlockSpec((d, bn), lambda i, j: (0, bz + j)),
                  pl.BlockSpec((bm, bn), lambda i, j: (i, j))],
        out_specs=pl.BlockSpec((bm, bn), lambda i, j: (i, j)),
        out_shape=jax.ShapeDtypeStruct((t, d_a), BF16),
        compiler_params=_params("parallel", "arbitrary"),
        name="inproj_gate_a",
    )(h, w_in, w_in, mixed)


def _chan_dft_kernel(h_ref, wx_ref, wz_ref, cs_ref, yc_ref, ys_ref, sz_ref, *, gd):
    h = h_ref[...]
    xb = _dot(h, wx_ref[...]).astype(BF16)
    for g in range(xb.shape[1] // gd):
        y = _dot(xb[:, g * gd:(g + 1) * gd], cs_ref[...])
        yc_ref[:, g * gd:(g + 1) * gd] = y[:, :gd].astype(yc_ref.dtype)
        ys_ref[:, g * gd:(g + 1) * gd] = y[:, gd:].astype(ys_ref.dtype)
    sz_ref[...] = jax.nn.silu(_dot(h, wz_ref[...])).astype(sz_ref.dtype)


def _chan_dft(h, w_in, col_x, col_z, d_b, cs_mat, bm, bn):
    t, d = h.shape
    gd = cs_mat.shape[0]
    bx, bz = col_x // bn, col_z // bn
    out = jax.ShapeDtypeStruct((t, d_b), BF16)
    tile = pl.BlockSpec((bm, bn), lambda i, j: (i, j))
    return pl.pallas_call(
        functools.partial(_chan_dft_kernel, gd=gd),
        grid=(t // bm, d_b // bn),
        in_specs=[pl.BlockSpec((bm, d), lambda i, j: (i, 0)),
                  pl.BlockSpec((d, bn), lambda i, j: (0, bx + j)),
                  pl.BlockSpec((d, bn), lambda i, j: (0, bz + j)),
                  pl.BlockSpec((gd, 2 * gd), lambda i, j: (0, 0))],
        out_specs=[tile, tile, tile],
        out_shape=[out, out, out],
        compiler_params=_params("parallel", "arbitrary"),
        name="inproj_chan_dft",
    )(h, w_in, w_in, cs_mat)


def _sig_kernel(h_ref, w_ref, b_ref, o_ref):
    o_ref[...] = jax.nn.sigmoid(_dot(h_ref[...], w_ref[...]) + b_ref[...]).astype(o_ref.dtype)


def _gates(h, w_in, col0, b_gate_flat, bm, bn):
    t, d = h.shape
    n = b_gate_flat.shape[1]
    blk0 = col0 // bn
    return pl.pallas_call(
        _sig_kernel,
        grid=(t // bm, n // bn),
        in_specs=[pl.BlockSpec((bm, d), lambda i, j: (i, 0)),
                  pl.BlockSpec((d, bn), lambda i, j: (0, blk0 + j)),
                  pl.BlockSpec((1, bn), lambda i, j: (0, j))],
        out_specs=pl.BlockSpec((bm, bn), lambda i, j: (i, j)),
        out_shape=jax.ShapeDtypeStruct((t, n), BF16),
        compiler_params=_params("parallel", "arbitrary"),
        name="inproj_gates",
    )(h, w_in, b_gate_flat)


def _seq_dft_kernel(cm_ref, sm_ref, yc_ref, ys_ref, sz_ref, o_ref):
    acc = _dot(cm_ref[...], yc_ref[...]) + _dot(sm_ref[...], ys_ref[...])
    o_ref[...] = (acc * sz_ref[...].astype(F32)).astype(o_ref.dtype)


def _seq_dft(yc, ys, sz, cos_m, nsin_m, seq, rt, ct):
    t, d_b = yc.shape
    nb = t // seq
    nr = seq // rt
    return pl.pallas_call(
        _seq_dft_kernel,
        grid=(nb, d_b // ct, nr),
        in_specs=[pl.BlockSpec((rt, seq), lambda b, c, r: (r, 0)),
                  pl.BlockSpec((rt, seq), lambda b, c, r: (r, 0)),
                  pl.BlockSpec((seq, ct), lambda b, c, r: (b, c)),
                  pl.BlockSpec((seq, ct), lambda b, c, r: (b, c)),
                  pl.BlockSpec((rt, ct), lambda b, c, r: (b * nr + r, c))],
        out_specs=pl.BlockSpec((rt, ct), lambda b, c, r: (b * nr + r, c)),
        out_shape=jax.ShapeDtypeStruct((t, d_b), BF16),
        compiler_params=_params("parallel", "parallel", "arbitrary"),
        name="seq_dft",
    )(cos_m, nsin_m, yc, ys, sz)


def _merge_kernel(a_ref, f_ref, wa_ref, wb_ref, sa_ref, sb_ref, o_ref):
    ya = _dot(a_ref[...], wa_ref[...])
    yb = _dot(f_ref[...], wb_ref[...])
    o_ref[...] = (sa_ref[...].astype(F32) * ya + sb_ref[...].astype(F32) * yb).astype(o_ref.dtype)


def _merge(a, fb, w_a, w_b, sg, bm, bn):
    t, d_a = a.shape
    d_b = fb.shape[1]
    d = w_a.shape[1]
    nd = d // bn
    return pl.pallas_call(
        _merge_kernel,
        grid=(t // bm, nd),
        in_specs=[pl.BlockSpec((bm, d_a), lambda i, j: (i, 0)),
                  pl.BlockSpec((bm, d_b), lambda i, j: (i, 0)),
                  pl.BlockSpec((d_a, bn), lambda i, j: (0, j)),
                  pl.BlockSpec((d_b, bn), lambda i, j: (0, j)),
                  pl.BlockSpec((bm, bn), lambda i, j: (i, j)),
                  pl.BlockSpec((bm, bn), lambda i, j: (i, nd + j))],
        out_specs=pl.BlockSpec((bm, bn), lambda i, j: (i, j)),
        out_shape=jax.ShapeDtypeStruct((t, d), BF16),
        compiler_params=_params("parallel", "arbitrary"),
        name="branch_merge",
    )(a, fb, w_a, w_b, sg, sg)


def _out_kernel(m_ref, w_ref, x_ref, o_ref):
    o_ref[...] = x_ref[...] + _dot(m_ref[...], w_ref[...])


def _out_proj(m, w_out, x, bm, bn):
    t, d = m.shape
    return pl.pallas_call(
        _out_kernel,
        grid=(t // bm, d // bn),
        in_specs=[pl.BlockSpec((bm, d), lambda i, j: (i, 0)),
                  pl.BlockSpec((d, bn), lambda i, j: (0, j)),
                  pl.BlockSpec((bm, bn), lambda i, j: (i, j))],
        out_specs=pl.BlockSpec((bm, bn), lambda i, j: (i, j)),
        out_shape=jax.ShapeDtypeStruct((t, d), F32),
        compiler_params=_params("parallel", "arbitrary"),
        name="out_proj",
    )(m, w_out, x)


def _dft_tables(n, scale):
    i = jnp.arange(n, dtype=jnp.int32)
    ang = ((i[:, None] * i[None, :]) % n).astype(F32) * np.float32(2.0 * np.pi / n)
    return jnp.cos(ang) * np.float32(scale), jnp.sin(ang) * np.float32(scale)


def _tile(n, pref):
    return pref if n % pref == 0 else n


def _trunk(x3, weights, seq_tables):
    (norm_g, w_in, sgu_ln_g, sgu_ln_b, w_s, b_s_full, w_a, w_b, b_gate, w_out, final_g, cs_mat) = weights
    bsz, seq, d = x3.shape
    t = bsz * seq
    depth = norm_g.shape[0]
    d_a, d_b = w_a.shape[1], w_b.shape[1]
    bm = _tile(t, ROW_TILE)
    bn = _tile(d_a, COL_TILE)
    cos_m, nsin_m = seq_tables
    rt = _tile(seq, 512)
    x = x3.reshape(t, d)
    for l in range(depth):
        h = _rmsnorm(x, norm_g[l], BF16, _tile(t, 512))
        mixed = _spatial(h, w_in[l], d_a, sgu_ln_g[l], sgu_ln_b[l], w_s[l], b_s_full[l], bm, bn)
        a = _gate_a(h, w_in[l], 0, 2 * d_a, mixed, bm, bn)
        yc, ys, sz = _chan_dft(h, w_in[l], 3 * d_a, 3 * d_a + d_b, d_b, cs_mat, bm, bn)
        sg = _gates(h, w_in[l], 3 * d_a + 2 * d_b, b_gate[l].reshape(1, 2 * d), bm, bn)
        fb = _seq_dft(yc, ys, sz, cos_m, nsin_m, seq, rt, bn)
        m = _merge(a, fb, w_a[l], w_b[l], sg, bm, bn)
        x = _out_proj(m, w_out[l], x, bm, bn)
    y = _rmsnorm(x, final_g, F32, _tile(t, 512))
    return y.reshape(bsz, seq, d)


def kernel(x_prompt, x_sample, norm_g, w_in, sgu_ln_g, sgu_ln_b, w_spatial, b_spatial, w_a, w_b, b_gate, w_out, final_g):
    d_b = w_b.shape[1]
    gd = d_b // B_GROUPS
    heads, chunk = b_spatial.shape[1], b_spatial.shape[2]
    hd = w_a.shape[1] // heads
    cc, sc = _dft_tables(gd, gd ** -0.5)
    cs_mat = jnp.concatenate([cc, sc], axis=1).astype(BF16)
    b_s_full = jnp.broadcast_to(b_spatial[..., None], b_spatial.shape + (hd,))
    weights = (norm_g, w_in.astype(BF16), sgu_ln_g, sgu_ln_b, w_spatial.astype(BF16), b_s_full,
               w_a.astype(BF16), w_b.astype(BF16), b_gate, w_out.astype(BF16), final_g, cs_mat)
    outs = []
    for x3 in (x_prompt, x_sample):
        seq = x3.shape[1]
        cs, sn = _dft_tables(seq, seq ** -0.5)
        outs.append(_trunk(x3, weights, (cs.astype(BF16), (-sn).astype(BF16))))
    return tuple(outs)
```

```python
import functools

import numpy as np
import jax
import jax.numpy as jnp
from jax import lax
from jax.experimental import pallas as pl
from jax.experimental.pallas import tpu as pltpu

B_GROUPS = 8
EPS = 1e-6
VMEM_LIMIT_BYTES = 56 * 1024 * 1024
ROW_TILE = 1024
COL_TILE = 512
ROW_SUB = 256
F32 = jnp.float32
BF16 = jnp.bfloat16


def _params(*sem):
    return pltpu.CompilerParams(dimension_semantics=sem, vmem_limit_bytes=VMEM_LIMIT_BYTES)


def _dot(a, b):
    return jnp.dot(a, b, preferred_element_type=F32)


def _row_blocks(n_rows):
    rs = ROW_SUB if n_rows % ROW_SUB == 0 else n_rows
    return [slice(r, r + rs) for r in range(0, n_rows, rs)]


def _tile(n, pref):
    return pref if n % pref == 0 else n


def _rmsnorm_kernel(x_ref, g_ref, o_ref):
    x = x_ref[...]
    ms = jnp.mean(x * x, axis=-1, keepdims=True)
    o_ref[...] = (x * lax.rsqrt(ms + EPS) * g_ref[...]).astype(o_ref.dtype)


def _rmsnorm(x, g, out_dtype, bm):
    t, d = x.shape
    return pl.pallas_call(
        _rmsnorm_kernel,
        grid=(t // bm,),
        in_specs=[pl.BlockSpec((bm, d), lambda i: (i, 0)),
                  pl.BlockSpec((1, d), lambda i: (0, 0))],
        out_specs=pl.BlockSpec((bm, d), lambda i: (i, 0)),
        out_shape=jax.ShapeDtypeStruct((t, d), out_dtype),
        compiler_params=_params("parallel"),
        name="rmsnorm",
    )(x, g.reshape(1, d))


def _spatial_kernel(h_ref, w_ref, lng_ref, lnb_ref, ws_ref, bs_ref, o_ref, gv_ref, *,
                    n_tiles, bn, chunk, hd, d_a):
    j = pl.program_id(1)
    for rows in _row_blocks(h_ref.shape[0]):
        gv_ref[j, rows, :] = jax.nn.gelu(_dot(h_ref[rows, :], w_ref[...]))

    @pl.when(j == n_tiles - 1)
    def _():
        n_chunks = gv_ref.shape[1] // chunk
        heads_per_tile = bn // hd

        def body(c, carry):
            rows = pl.ds(pl.multiple_of(c * chunk, chunk), chunk)
            tiles = [gv_ref[t, rows, :] for t in range(n_tiles)]
            mu = sum(jnp.sum(x, axis=-1, keepdims=True) for x in tiles) / d_a
            cent = [x - mu for x in tiles]
            var = sum(jnp.sum(x * x, axis=-1, keepdims=True) for x in cent) / d_a
            rstd = lax.rsqrt(var + EPS)
            for t in range(n_tiles):
                cols = slice(t * bn, (t + 1) * bn)
                vn = (cent[t] * rstd * lng_ref[:, cols] + lnb_ref[:, cols]).astype(BF16)
                for hh in range(heads_per_tile):
                    head = t * heads_per_tile + hh
                    mixed = _dot(ws_ref[head], vn[:, hh * hd:(hh + 1) * hd]) + bs_ref[head]
                    o_ref[rows, head * hd:(head + 1) * hd] = mixed.astype(o_ref.dtype)
            return carry

        lax.fori_loop(0, n_chunks, body, 0)


def _spatial(h, w_in, col0, ln_g, ln_b, w_s, b_s_full, bm, bn):
    t, d = h.shape
    heads, chunk, _ = w_s.shape
    d_a = ln_g.shape[0]
    hd = d_a // heads
    n_tiles = d_a // bn
    blk0 = col0 // bn
    kern = functools.partial(_spatial_kernel, n_tiles=n_tiles, bn=bn, chunk=chunk, hd=hd, d_a=d_a)
    return pl.pallas_call(
        kern,
        grid=(t // bm, n_tiles),
        in_specs=[pl.BlockSpec((bm, d), lambda i, j: (i, 0)),
                  pl.BlockSpec((d, bn), lambda i, j: (0, blk0 + j)),
                  pl.BlockSpec((1, d_a), lambda i, j: (0, 0)),
                  pl.BlockSpec((1, d_a), lambda i, j: (0, 0)),
                  pl.BlockSpec((heads, chunk, chunk), lambda i, j: (0, 0, 0)),
                  pl.BlockSpec((heads, chunk, hd), lambda i, j: (0, 0, 0))],
        out_specs=pl.BlockSpec((bm, d_a), lambda i, j: (i, 0)),
        out_shape=jax.ShapeDtypeStruct((t, d_a), BF16),
        scratch_shapes=[pltpu.VMEM((n_tiles, bm, bn), F32)],
        compiler_params=_params("parallel", "arbitrary"),
        name="inproj_spatial",
    )(h, w_in, ln_g.reshape(1, d_a), ln_b.reshape(1, d_a), w_s, b_s_full)


def _gate_a_kernel(h_ref, wu_ref, wz_ref, mixed_ref, o_ref):
    for rows in _row_blocks(h_ref.shape[0]):
        h = h_ref[rows, :]
        u = jax.nn.gelu(_dot(h, wu_ref[...]))
        z = jax.nn.silu(_dot(h, wz_ref[...]))
        o_ref[rows, :] = (u * mixed_ref[rows, :].astype(F32) * z).astype(o_ref.dtype)


def _gate_a(h, w_in, col_u, col_z, mixed, bm, bn):
    t, d = h.shape
    d_a = mixed.shape[1]
    bu, bz = col_u // bn, col_z // bn
    return pl.pallas_call(
        _gate_a_kernel,
        grid=(t // bm, d_a // bn),
        in_specs=[pl.BlockSpec((bm, d), lambda i, j: (i, 0)),
                  pl.BlockSpec((d, bn), lambda i, j: (0, bu + j)),
                  pl.BlockSpec((d, bn), lambda i, j: (0, bz + j)),
                  pl.BlockSpec((bm, bn), lambda i, j: (i, j))],
        out_specs=pl.BlockSpec((bm, bn), lambda i, j: (i, j)),
        out_shape=jax.ShapeDtypeStruct((t, d_a), BF16),
        compiler_params=_params("parallel", "arbitrary"),
        name="inproj_gate_a",
    )(h, w_in, w_in, mixed)


def _chan_dft_kernel(h_ref, wx_ref, wz_ref, cs_ref, yc_ref, ys_ref, sz_ref, *, gd):
    for rows in _row_blocks(h_ref.shape[0]):
        h = h_ref[rows, :]
        xb = _dot(h, wx_ref[...]).astype(BF16)
        for g in range(xb.shape[1] // gd):
            cols = slice(g * gd, (g + 1) * gd)
            y = _dot(xb[:, cols], cs_ref[...])
            yc_ref[rows, cols] = y[:, :gd].astype(yc_ref.dtype)
            ys_ref[rows, cols] = y[:, gd:].astype(ys_ref.dtype)
        sz_ref[rows, :] = jax.nn.silu(_dot(h, wz_ref[...])).astype(sz_ref.dtype)


def _chan_dft(h, w_in, col_x, col_z, d_b, cs_mat, bm, bn):
    t, d = h.shape
    gd = cs_mat.shape[0]
    bx, bz = col_x // bn, col_z // bn
    out = jax.ShapeDtypeStruct((t, d_b), BF16)
    tile = pl.BlockSpec((bm, bn), lambda i, j: (i, j))
    return pl.pallas_call(
        functools.partial(_chan_dft_kernel, gd=gd),
        grid=(t // bm, d_b // bn),
        in_specs=[pl.BlockSpec((bm, d), lambda i, j: (i, 0)),
                  pl.BlockSpec((d, bn), lambda i, j: (0, bx + j)),
                  pl.BlockSpec((d, bn), lambda i, j: (0, bz + j)),
                  pl.BlockSpec((gd, 2 * gd), lambda i, j: (0, 0))],
        out_specs=[tile, tile, tile],
        out_shape=[out, out, out],
        compiler_params=_params("parallel", "arbitrary"),
        name="inproj_chan_dft",
    )(h, w_in, w_in, cs_mat)


def _sig_kernel(h_ref, w_ref, b_ref, o_ref):
    for rows in _row_blocks(h_ref.shape[0]):
        g = _dot(h_ref[rows, :], w_ref[...]) + b_ref[...]
        o_ref[rows, :] = jax.nn.sigmoid(g).astype(o_ref.dtype)


def _gates(h, w_in, col0, b_gate_flat, bm, bn):
    t, d = h.shape
    n = b_gate_flat.shape[1]
    blk0 = col0 // bn
    return pl.pallas_call(
        _sig_kernel,
        grid=(t // bm, n // bn),
        in_specs=[pl.BlockSpec((bm, d), lambda i, j: (i, 0)),
                  pl.BlockSpec((d, bn), lambda i, j: (0, blk0 + j)),
                  pl.BlockSpec((1, bn), lambda i, j: (0, j))],
        out_specs=pl.BlockSpec((bm, bn), lambda i, j: (i, j)),
        out_shape=jax.ShapeDtypeStruct((t, n), BF16),
        compiler_params=_params("parallel", "arbitrary"),
        name="inproj_gates",
    )(h, w_in, b_gate_flat)


def _seq_dft_kernel(cm_ref, sm_ref, yc_ref, ys_ref, sz_ref, o_ref):
    for rows in _row_blocks(cm_ref.shape[0]):
        acc = _dot(cm_ref[rows, :], yc_ref[...]) + _dot(sm_ref[rows, :], ys_ref[...])
        o_ref[rows, :] = (acc * sz_ref[rows, :].astype(F32)).astype(o_ref.dtype)


def _seq_dft(yc, ys, sz, cos_m, nsin_m, seq, rt, ct):
    t, d_b = yc.shape
    nb = t // seq
    nr = seq // rt
    return pl.pallas_call(
        _seq_dft_kernel,
        grid=(nb, d_b // ct, nr),
        in_specs=[pl.BlockSpec((rt, seq), lambda b, c, r: (r, 0)),
                  pl.BlockSpec((rt, seq), lambda b, c, r: (r, 0)),
                  pl.BlockSpec((seq, ct), lambda b, c, r: (b, c)),
                  pl.BlockSpec((seq, ct), lambda b, c, r: (b, c)),
                  pl.BlockSpec((rt, ct), lambda b, c, r: (b * nr + r, c))],
        out_specs=pl.BlockSpec((rt, ct), lambda b, c, r: (b * nr + r, c)),
        out_shape=jax.ShapeDtypeStruct((t, d_b), BF16),
        compiler_params=_params("parallel", "parallel", "arbitrary"),
        name="seq_dft",
    )(cos_m, nsin_m, yc, ys, sz)


def _merge_kernel(a_ref, f_ref, wa_ref, wb_ref, sa_ref, sb_ref, o_ref):
    for rows in _row_blocks(a_ref.shape[0]):
        ya = _dot(a_ref[rows, :], wa_ref[...])
        yb = _dot(f_ref[rows, :], wb_ref[...])
        m = sa_ref[rows, :].astype(F32) * ya + sb_ref[rows, :].astype(F32) * yb
        o_ref[rows, :] = m.astype(o_ref.dtype)


def _merge(a, fb, w_a, w_b, sg, bm, bn):
    t, d_a = a.shape
    d_b = fb.shape[1]
    d = w_a.shape[1]
    nd = d // bn
    return pl.pallas_call(
        _merge_kernel,
        grid=(t // bm, nd),
        in_specs=[pl.BlockSpec((bm, d_a), lambda i, j: (i, 0)),
                  pl.BlockSpec((bm, d_b), lambda i, j: (i, 0)),
                  pl.BlockSpec((d_a, bn), lambda i, j: (0, j)),
                  pl.BlockSpec((d_b, bn), lambda i, j: (0, j)),
                  pl.BlockSpec((bm, bn), lambda i, j: (i, j)),
                  pl.BlockSpec((bm, bn), lambda i, j: (i, nd + j))],
        out_specs=pl.BlockSpec((bm, bn), lambda i, j: (i, j)),
        out_shape=jax.ShapeDtypeStruct((t, d), BF16),
        compiler_params=_params("parallel", "arbitrary"),
        name="branch_merge",
    )(a, fb, w_a, w_b, sg, sg)


def _out_kernel(m_ref, w_ref, x_ref, o_ref):
    for rows in _row_blocks(m_ref.shape[0]):
        o_ref[rows, :] = x_ref[rows, :] + _dot(m_ref[rows, :], w_ref[...])


def _out_proj(m, w_out, x, bm, bn):
    t, d = m.shape
    return pl.pallas_call(
        _out_kernel,
        grid=(t // bm, d // bn),
        in_specs=[pl.BlockSpec((bm, d), lambda i, j: (i, 0)),
                  pl.BlockSpec((d, bn), lambda i, j: (0, j)),
                  pl.BlockSpec((bm, bn), lambda i, j: (i, j))],
        out_specs=pl.BlockSpec((bm, bn), lambda i, j: (i, j)),
        out_shape=jax.ShapeDtypeStruct((t, d), F32),
        compiler_params=_params("parallel", "arbitrary"),
        name="out_proj",
    )(m, w_out, x)


def _dft_tables(n, scale):
    i = jnp.arange(n, dtype=jnp.int32)
    ang = ((i[:, None] * i[None, :]) % n).astype(F32) * np.float32(2.0 * np.pi / n)
    return jnp.cos(ang) * np.float32(scale), jnp.sin(ang) * np.float32(scale)


def _trunk(x3, weights, seq_tables):
    (norm_g, w_in, sgu_ln_g, sgu_ln_b, w_s, b_s_full, w_a, w_b, b_gate, w_out, final_g, cs_mat) = weights
    bsz, seq, d = x3.shape
    t = bsz * seq
    d_a, d_b = w_a[0].shape[0], w_b[0].shape[0]
    bm = _tile(t, ROW_TILE)
    bn = _tile(d_a, COL_TILE)
    bn2 = _tile(d_a, 2 * COL_TILE)
    cos_m, nsin_m = seq_tables
    rt = _tile(seq, 512 if seq > 2048 else 1024)
    x = x3.reshape(t, d)
    for l in range(len(w_in)):
        h = _rmsnorm(x, norm_g[l], BF16, _tile(t, 512))
        mixed = _spatial(h, w_in[l], d_a, sgu_ln_g[l], sgu_ln_b[l], w_s[l], b_s_full[l], bm, bn2)
        a = _gate_a(h, w_in[l], 0, 2 * d_a, mixed, bm, bn)
        yc, ys, sz = _chan_dft(h, w_in[l], 3 * d_a, 3 * d_a + d_b, d_b, cs_mat, bm, bn)
        sg = _gates(h, w_in[l], 3 * d_a + 2 * d_b, b_gate[l].reshape(1, 2 * d), bm, bn2)
        fb = _seq_dft(yc, ys, sz, cos_m, nsin_m, seq, rt, bn)
        m = _merge(a, fb, w_a[l], w_b[l], sg, bm, bn2)
        x = _out_proj(m, w_out[l], x, bm, bn)
    y = _rmsnorm(x, final_g, F32, _tile(t, 512))
    return y.reshape(bsz, seq, d)


def kernel(x_prompt, x_sample, norm_g, w_in, sgu_ln_g, sgu_ln_b, w_spatial, b_spatial, w_a, w_b, b_gate, w_out, final_g):
    depth = norm_g.shape[0]
    d_b = w_b.shape[1]
    gd = d_b // B_GROUPS
    heads = b_spatial.shape[1]
    hd = w_a.shape[1] // heads
    cc, sc = _dft_tables(gd, gd ** -0.5)
    cs_mat = jnp.concatenate([cc, sc], axis=1).astype(BF16)
    b_s_full = jnp.broadcast_to(b_spatial[..., None], b_spatial.shape + (hd,))
    per_layer_bf16 = lambda w: [w[l].astype(BF16) for l in range(depth)]
    weights = (norm_g, per_layer_bf16(w_in), sgu_ln_g, sgu_ln_b, w_spatial.astype(BF16), b_s_full,
               per_layer_bf16(w_a), per_layer_bf16(w_b), b_gate, per_layer_bf16(w_out), final_g, cs_mat)
    outs = []
    for x3 in (x_prompt, x_sample):
        seq = x3.shape[1]
        cs, sn = _dft_tables(seq, seq ** -0.5)
        outs.append(_trunk(x3, weights, (cs.astype(BF16), (-sn).astype(BF16))))
    return tuple(outs)
```

```python
import functools

import numpy as np
import jax
import jax.numpy as jnp
from jax import lax
from jax.experimental import pallas as pl
from jax.experimental.pallas import tpu as pltpu

B_GROUPS = 8
EPS = 1e-6
VMEM_LIMIT_BYTES = 56 * 1024 * 1024
ROW_TILE = 1024
COL_TILE = 512
ROW_SUB = 256
NORM_ROWS = 64
F32 = jnp.float32
BF16 = jnp.bfloat16


def _params(*sem):
    return pltpu.CompilerParams(dimension_semantics=sem, vmem_limit_bytes=VMEM_LIMIT_BYTES)


def _dot(a, b):
    return jnp.dot(a, b, preferred_element_type=F32)


def _row_blocks(n_rows):
    rs = ROW_SUB if n_rows % ROW_SUB == 0 else n_rows
    return [slice(r, r + rs) for r in range(0, n_rows, rs)]


def _tile(n, pref):
    return pref if n % pref == 0 else n


def _rmsnorm_kernel(x_ref, g_ref, o_ref):
    x = x_ref[...]
    ms = jnp.mean(x * x, axis=-1, keepdims=True)
    o_ref[...] = (x * lax.rsqrt(ms + EPS) * g_ref[...]).astype(o_ref.dtype)


def _rmsnorm(x, g, out_dtype, bm):
    t, d = x.shape
    return pl.pallas_call(
        _rmsnorm_kernel,
        grid=(t // bm,),
        in_specs=[pl.BlockSpec((bm, d), lambda i: (i, 0)),
                  pl.BlockSpec((1, d), lambda i: (0, 0))],
        out_specs=pl.BlockSpec((bm, d), lambda i: (i, 0)),
        out_shape=jax.ShapeDtypeStruct((t, d), out_dtype),
        compiler_params=_params("parallel"),
        name="rmsnorm",
    )(x, g.reshape(1, d))


def _spatial_kernel(h_ref, w_ref, lng_ref, lnb_ref, ws_ref, bs_ref, o_ref, gv_ref, *,
                    n_tiles, bn, chunk, hd, d_a):
    j = pl.program_id(1)
    for rows in _row_blocks(h_ref.shape[0]):
        gv_ref[j, rows, :] = jax.nn.gelu(_dot(h_ref[rows, :], w_ref[...]))

    @pl.when(j == n_tiles - 1)
    def _():
        n_chunks = gv_ref.shape[1] // chunk
        heads_per_tile = bn // hd

        def body(c, carry):
            rows = pl.ds(pl.multiple_of(c * chunk, chunk), chunk)
            tiles = [gv_ref[t, rows, :] for t in range(n_tiles)]
            mu = sum(jnp.sum(x, axis=-1, keepdims=True) for x in tiles) / d_a
            cent = [x - mu for x in tiles]
            var = sum(jnp.sum(x * x, axis=-1, keepdims=True) for x in cent) / d_a
            rstd = lax.rsqrt(var + EPS)
            for t in range(n_tiles):
                cols = slice(t * bn, (t + 1) * bn)
                vn = (cent[t] * rstd * lng_ref[:, cols] + lnb_ref[:, cols]).astype(BF16)
                for hh in range(heads_per_tile):
                    head = t * heads_per_tile + hh
                    mixed = _dot(ws_ref[head], vn[:, hh * hd:(hh + 1) * hd]) + bs_ref[head]
                    o_ref[rows, head * hd:(head + 1) * hd] = mixed.astype(o_ref.dtype)
            return carry

        lax.fori_loop(0, n_chunks, body, 0)


def _spatial(h, w_in, layer, col0, ln_g, ln_b, w_s, b_s_full, bm, bn):
    t, d = h.shape
    _, heads, chunk, _ = w_s.shape
    d_a = ln_g.shape[0]
    hd = d_a // heads
    n_tiles = d_a // bn
    blk0 = col0 // bn
    kern = functools.partial(_spatial_kernel, n_tiles=n_tiles, bn=bn, chunk=chunk, hd=hd, d_a=d_a)
    return pl.pallas_call(
        kern,
        grid=(t // bm, n_tiles),
        in_specs=[pl.BlockSpec((bm, d), lambda i, j: (i, 0)),
                  pl.BlockSpec((None, d, bn), lambda i, j: (layer, 0, blk0 + j)),
                  pl.BlockSpec((1, d_a), lambda i, j: (0, 0)),
                  pl.BlockSpec((1, d_a), lambda i, j: (0, 0)),
                  pl.BlockSpec((None, heads, chunk, chunk), lambda i, j: (layer, 0, 0, 0)),
                  pl.BlockSpec((None, heads, chunk, hd), lambda i, j: (layer, 0, 0, 0))],
        out_specs=pl.BlockSpec((bm, d_a), lambda i, j: (i, 0)),
        out_shape=jax.ShapeDtypeStruct((t, d_a), BF16),
        scratch_shapes=[pltpu.VMEM((n_tiles, bm, bn), F32)],
        compiler_params=_params("parallel", "arbitrary"),
        name="inproj_spatial",
    )(h, w_in, ln_g.reshape(1, d_a), ln_b.reshape(1, d_a), w_s, b_s_full)


def _gate_a_kernel(h_ref, wu_ref, wz_ref, mixed_ref, o_ref):
    for rows in _row_blocks(h_ref.shape[0]):
        h = h_ref[rows, :]
        u = jax.nn.gelu(_dot(h, wu_ref[...]))
        z = jax.nn.silu(_dot(h, wz_ref[...]))
        o_ref[rows, :] = (u * mixed_ref[rows, :].astype(F32) * z).astype(o_ref.dtype)


def _gate_a(h, w_in, layer, col_u, col_z, mixed, bm, bn):
    t, d = h.shape
    d_a = mixed.shape[1]
    bu, bz = col_u // bn, col_z // bn
    return pl.pallas_call(
        _gate_a_kernel,
        grid=(t // bm, d_a // bn),
        in_specs=[pl.BlockSpec((bm, d), lambda i, j: (i, 0)),
                  pl.BlockSpec((None, d, bn), lambda i, j: (layer, 0, bu + j)),
                  pl.BlockSpec((None, d, bn), lambda i, j: (layer, 0, bz + j)),
                  pl.BlockSpec((bm, bn), lambda i, j: (i, j))],
        out_specs=pl.BlockSpec((bm, bn), lambda i, j: (i, j)),
        out_shape=jax.ShapeDtypeStruct((t, d_a), BF16),
        compiler_params=_params("parallel", "arbitrary"),
        name="inproj_gate_a",
    )(h, w_in, w_in, mixed)


def _chan_dft_kernel(h_ref, wx_ref, wz_ref, cs_ref, yc_ref, ys_ref, sz_ref, *, gd):
    for rows in _row_blocks(h_ref.shape[0]):
        h = h_ref[rows, :]
        xb = _dot(h, wx_ref[...]).astype(BF16)
        for g in range(xb.shape[1] // gd):
            cols = slice(g * gd, (g + 1) * gd)
            y = _dot(xb[:, cols], cs_ref[...])
            yc_ref[rows, cols] = y[:, :gd].astype(yc_ref.dtype)
            ys_ref[rows, cols] = y[:, gd:].astype(ys_ref.dtype)
        sz_ref[rows, :] = jax.nn.silu(_dot(h, wz_ref[...])).astype(sz_ref.dtype)


def _chan_dft(h, w_in, layer, col_x, col_z, d_b, cs_mat, bm, bn):
    t, d = h.shape
    gd = cs_mat.shape[0]
    bx, bz = col_x // bn, col_z // bn
    out = jax.ShapeDtypeStruct((t, d_b), BF16)
    tile = pl.BlockSpec((bm, bn), lambda i, j: (i, j))
    return pl.pallas_call(
        functools.partial(_chan_dft_kernel, gd=gd),
        grid=(t // bm, d_b // bn),
        in_specs=[pl.BlockSpec((bm, d), lambda i, j: (i, 0)),
                  pl.BlockSpec((None, d, bn), lambda i, j: (layer, 0, bx + j)),
                  pl.BlockSpec((None, d, bn), lambda i, j: (layer, 0, bz + j)),
                  pl.BlockSpec((gd, 2 * gd), lambda i, j: (0, 0))],
        out_specs=[tile, tile, tile],
        out_shape=[out, out, out],
        compiler_params=_params("parallel", "arbitrary"),
        name="inproj_chan_dft",
    )(h, w_in, w_in, cs_mat)


def _sig_kernel(h_ref, w_ref, b_ref, o_ref):
    for rows in _row_blocks(h_ref.shape[0]):
        g = _dot(h_ref[rows, :], w_ref[...]) + b_ref[...]
        o_ref[rows, :] = jax.nn.sigmoid(g).astype(o_ref.dtype)


def _gates(h, w_in, layer, col0, b_gate_flat, bm, bn):
    t, d = h.shape
    n = b_gate_flat.shape[1]
    blk0 = col0 // bn
    return pl.pallas_call(
        _sig_kernel,
        grid=(t // bm, n // bn),
        in_specs=[pl.BlockSpec((bm, d), lambda i, j: (i, 0)),
                  pl.BlockSpec((None, d, bn), lambda i, j: (layer, 0, blk0 + j)),
                  pl.BlockSpec((1, bn), lambda i, j: (0, j))],
        out_specs=pl.BlockSpec((bm, bn), lambda i, j: (i, j)),
        out_shape=jax.ShapeDtypeStruct((t, n), BF16),
        compiler_params=_params("parallel", "arbitrary"),
        name="inproj_gates",
    )(h, w_in, b_gate_flat)


def _seq_dft_kernel(ce_ref, se_ref, co_ref, so_ref, yce_ref, yse_ref, yco_ref, yso_ref, sz_ref, o_ref):
    for rows in _row_blocks(ce_ref.shape[0]):
        ev = _dot(ce_ref[rows, :], yce_ref[...]) + _dot(se_ref[rows, :], yse_ref[...])
        od = _dot(co_ref[rows, :], yco_ref[...]) + _dot(so_ref[rows, :], yso_ref[...])
        o_ref[0, rows, :] = ((ev + od) * sz_ref[0, rows, :].astype(F32)).astype(o_ref.dtype)
        o_ref[1, rows, :] = ((ev - od) * sz_ref[1, rows, :].astype(F32)).astype(o_ref.dtype)


def _seq_dft(yc, ys, sz, tables, seq, rt, ct):
    t, d_b = yc.shape
    nb, half = t // seq, seq // 2
    nct = d_b // ct
    yc2 = yc.reshape(t // 2, 2 * d_b)
    ys2 = ys.reshape(t // 2, 2 * d_b)
    sz4 = sz.reshape(nb, 2, half, d_b)
    tab = pl.BlockSpec((rt, half), lambda b, c, r: (r, 0))
    even = pl.BlockSpec((half, ct), lambda b, c, r: (b, c))
    odd = pl.BlockSpec((half, ct), lambda b, c, r: (b, nct + c))
    halves = pl.BlockSpec((None, 2, rt, ct), lambda b, c, r: (b, 0, r, c))
    out = pl.pallas_call(
        _seq_dft_kernel,
        grid=(nb, nct, half // rt),
        in_specs=[tab, tab, tab, tab, even, even, odd, odd, halves],
        out_specs=halves,
        out_shape=jax.ShapeDtypeStruct((nb, 2, half, d_b), BF16),
        compiler_params=_params("parallel", "parallel", "arbitrary"),
        name="seq_dft",
    )(*tables, yc2, ys2, yc2, ys2, sz4)
    return out.reshape(t, d_b)


def _merge_kernel(a_ref, f_ref, wa_ref, wb_ref, sa_ref, sb_ref, o_ref):
    for rows in _row_blocks(a_ref.shape[0]):
        ya = _dot(a_ref[rows, :], wa_ref[...])
        yb = _dot(f_ref[rows, :], wb_ref[...])
        m = sa_ref[rows, :].astype(F32) * ya + sb_ref[rows, :].astype(F32) * yb
        o_ref[rows, :] = m.astype(o_ref.dtype)


def _merge(a, fb, w_a, w_b, layer, sg, bm, bn):
    t, d_a = a.shape
    d_b = fb.shape[1]
    d = w_a.shape[2]
    nd = d // bn
    return pl.pallas_call(
        _merge_kernel,
        grid=(t // bm, nd),
        in_specs=[pl.BlockSpec((bm, d_a), lambda i, j: (i, 0)),
                  pl.BlockSpec((bm, d_b), lambda i, j: (i, 0)),
                  pl.BlockSpec((None, d_a, bn), lambda i, j: (layer, 0, j)),
                  pl.BlockSpec((None, d_b, bn), lambda i, j: (layer, 0, j)),
                  pl.BlockSpec((bm, bn), lambda i, j: (i, j)),
                  pl.BlockSpec((bm, bn), lambda i, j: (i, nd + j))],
        out_specs=pl.BlockSpec((bm, bn), lambda i, j: (i, j)),
        out_shape=jax.ShapeDtypeStruct((t, d), BF16),
        compiler_params=_params("parallel", "arbitrary"),
        name="branch_merge",
    )(a, fb, w_a, w_b, sg, sg)


def _out_norm_kernel(m_ref, w_ref, x_ref, g_ref, *refs, n_tiles, bn, d, emit_x):
    if emit_x:
        xo_ref, y_ref, xs_ref = refs
    else:
        y_ref, xs_ref = refs
    j = pl.program_id(1)
    for rows in _row_blocks(m_ref.shape[0]):
        xn = x_ref[rows, :] + _dot(m_ref[rows, :], w_ref[...])
        xs_ref[j, rows, :] = xn
        if emit_x:
            xo_ref[rows, :] = xn

    @pl.when(j == n_tiles - 1)
    def _():
        nr = NORM_ROWS if xs_ref.shape[1] % NORM_ROWS == 0 else xs_ref.shape[1]

        def body(c, carry):
            rows = pl.ds(pl.multiple_of(c * nr, nr), nr)
            ss = jnp.zeros((nr, 1), F32)
            for t in range(n_tiles):
                xt = xs_ref[t, rows, :]
                ss = ss + jnp.sum(xt * xt, axis=-1, keepdims=True)
            scale = lax.rsqrt(ss / d + EPS)
            for t in range(n_tiles):
                cols = slice(t * bn, (t + 1) * bn)
                y_ref[rows, cols] = (xs_ref[t, rows, :] * scale * g_ref[:, cols]).astype(y_ref.dtype)
            return carry

        lax.fori_loop(0, xs_ref.shape[1] // nr, body, 0)


def _out_proj_norm(m, w_out, layer, x, g, bm, bn, emit_x, y_dtype):
    t, d = m.shape
    n_tiles = d // bn
    tile = pl.BlockSpec((bm, bn), lambda i, j: (i, j))
    rows_full = pl.BlockSpec((bm, d), lambda i, j: (i, 0))
    y_shape = jax.ShapeDtypeStruct((t, d), y_dtype)
    kern = functools.partial(_out_norm_kernel, n_tiles=n_tiles, bn=bn, d=d, emit_x=emit_x)
    return pl.pallas_call(
        kern,
        grid=(t // bm, n_tiles),
        in_specs=[rows_full,
                  pl.BlockSpec((None, d, bn), lambda i, j: (layer, 0, j)),
                  tile,
                  pl.BlockSpec((1, d), lambda i, j: (0, 0))],
        out_specs=[tile, rows_full] if emit_x else rows_full,
        out_shape=[jax.ShapeDtypeStruct((t, d), F32), y_shape] if emit_x else y_shape,
        scratch_shapes=[pltpu.VMEM((n_tiles, bm, bn), F32)],
        compiler_params=_params("parallel", "arbitrary"),
        name="out_proj_norm",
    )(m, w_out, x, g.reshape(1, d))


def _cos_sin(k, m, n, scale):
    ang = ((k * m) % n).astype(F32) * np.float32(2.0 * np.pi / n)
    return jnp.cos(ang) * np.float32(scale), jnp.sin(ang) * np.float32(-scale)


def _seq_tables(seq):
    half = seq // 2
    k = jnp.arange(half, dtype=jnp.int32)[:, None]
    m = jnp.arange(half, dtype=jnp.int32)[None, :]
    ce, se = _cos_sin(k, 2 * m, seq, seq ** -0.5)
    co, so = _cos_sin(k, 2 * m + 1, seq, seq ** -0.5)
    return tuple(a.astype(BF16) for a in (ce, se, co, so))


def _trunk(x3, weights, tables):
    (norm_g, w_in, sgu_ln_g, sgu_ln_b, w_s, b_s_full, w_a, w_b, b_gate, w_out, final_g, cs_mat) = weights
    bsz, seq, d = x3.shape
    t = bsz * seq
    depth, d_a, d_b = w_a.shape[0], w_a.shape[1], w_b.shape[1]
    bm = _tile(t, ROW_TILE)
    bn = _tile(d_a, COL_TILE)
    bn2 = _tile(d_a, 2 * COL_TILE)
    rt = _tile(seq // 2, 512 if seq > 2048 else 1024)
    x = x3.reshape(t, d)
    h = _rmsnorm(x, norm_g[0], BF16, _tile(t, 512))
    for l in range(depth):
        mixed = _spatial(h, w_in, l, d_a, sgu_ln_g[l], sgu_ln_b[l], w_s, b_s_full, bm, bn2)
        a = _gate_a(h, w_in, l, 0, 2 * d_a, mixed, bm, bn)
        yc, ys, sz = _chan_dft(h, w_in, l, 3 * d_a, 3 * d_a + d_b, d_b, cs_mat, bm, bn)
        sg = _gates(h, w_in, l, 3 * d_a + 2 * d_b, b_gate[l].reshape(1, 2 * d), bm, bn2)
        fb = _seq_dft(yc, ys, sz, tables, seq, rt, bn)
        m = _merge(a, fb, w_a, w_b, l, sg, bm, bn2)
        if l + 1 < depth:
            x, h = _out_proj_norm(m, w_out, l, x, norm_g[l + 1], _tile(t, 512), bn, True, BF16)
        else:
            y = _out_proj_norm(m, w_out, l, x, final_g, _tile(t, 512), bn, False, F32)
    return y.reshape(bsz, seq, d)


def kernel(x_prompt, x_sample, norm_g, w_in, sgu_ln_g, sgu_ln_b, w_spatial, b_spatial, w_a, w_b, b_gate, w_out, final_g):
    d_b = w_b.shape[1]
    gd = d_b // B_GROUPS
    hd = w_a.shape[1] // b_spatial.shape[1]
    i = jnp.arange(gd, dtype=jnp.int32)
    cc, nsc = _cos_sin(i[:, None], i[None, :], gd, gd ** -0.5)
    cs_mat = jnp.concatenate([cc, -nsc], axis=1).astype(BF16)
    b_s_full = jnp.broadcast_to(b_spatial[..., None], b_spatial.shape + (hd,))
    weights = (norm_g, w_in.astype(BF16), sgu_ln_g, sgu_ln_b, w_spatial.astype(BF16), b_s_full,
               w_a.astype(BF16), w_b.astype(BF16), b_gate, w_out.astype(BF16), final_g, cs_mat)
    return tuple(_trunk(x3, weights, _seq_tables(x3.shape[1])) for x3 in (x_prompt, x_sample))
```

```python
import functools

import numpy as np
import jax
import jax.numpy as jnp
from jax import lax
from jax.experimental import pallas as pl
from jax.experimental.pallas import tpu as pltpu

B_GROUPS = 8
EPS = 1e-6
VMEM_LIMIT_BYTES = 56 * 1024 * 1024
ROW_TILE = 1024
COL_TILE = 512
ROW_SUB = 256
LANES = 128
F32 = jnp.float32
BF16 = jnp.bfloat16


def _params(*sem):
    return pltpu.CompilerParams(dimension_semantics=sem, vmem_limit_bytes=VMEM_LIMIT_BYTES)


def _dot(a, b):
    return jnp.dot(a, b, preferred_element_type=F32)


def _row_blocks(n_rows):
    rs = ROW_SUB if n_rows % ROW_SUB == 0 else n_rows
    return [slice(r, r + rs) for r in range(0, n_rows, rs)]


def _tile(n, pref):
    return pref if n % pref == 0 else n


def _rowscale(p, r):
    reps = p.shape[1] // r.shape[1]
    return p * (jnp.concatenate([r] * reps, axis=1) if reps > 1 else r)


def _prep_kernel(x_ref, xb_ref, r_ref):
    x = x_ref[...]
    xb_ref[...] = x.astype(xb_ref.dtype)
    ms = jnp.mean(x * x, axis=-1, keepdims=True)
    r_ref[...] = jnp.broadcast_to(lax.rsqrt(ms + EPS), r_ref.shape)


def _prep(x, bm):
    t, d = x.shape
    return pl.pallas_call(
        _prep_kernel,
        grid=(t // bm,),
        in_specs=[pl.BlockSpec((bm, d), lambda i: (i, 0))],
        out_specs=[pl.BlockSpec((bm, d), lambda i: (i, 0)), pl.BlockSpec((bm, LANES), lambda i: (i, 0))],
        out_shape=[jax.ShapeDtypeStruct((t, d), BF16), jax.ShapeDtypeStruct((t, LANES), F32)],
        compiler_params=_params("parallel"),
        name="prenorm_stats",
    )(x)


def _rmsnorm_kernel(x_ref, g_ref, o_ref):
    x = x_ref[...]
    ms = jnp.mean(x * x, axis=-1, keepdims=True)
    o_ref[...] = (x * lax.rsqrt(ms + EPS) * g_ref[...]).astype(o_ref.dtype)


def _rmsnorm(x, g, bm):
    t, d = x.shape
    return pl.pallas_call(
        _rmsnorm_kernel,
        grid=(t // bm,),
        in_specs=[pl.BlockSpec((bm, d), lambda i: (i, 0)),
                  pl.BlockSpec((1, d), lambda i: (0, 0))],
        out_specs=pl.BlockSpec((bm, d), lambda i: (i, 0)),
        out_shape=jax.ShapeDtypeStruct((t, d), x.dtype),
        compiler_params=_params("parallel"),
        name="final_rmsnorm",
    )(x, g.reshape(1, d))


def _spatial_kernel(h_ref, r_ref, w_ref, lng_ref, lnb_ref, ws_ref, bs_ref, o_ref, gv_ref, *,
                    n_tiles, bn, chunk, hd, d_a):
    j = pl.program_id(1)
    for rows in _row_blocks(h_ref.shape[0]):
        gv_ref[j, rows, :] = jax.nn.gelu(_rowscale(_dot(h_ref[rows, :], w_ref[...]), r_ref[rows, :]))

    @pl.when(j == n_tiles - 1)
    def _():
        n_chunks = gv_ref.shape[1] // chunk
        heads_per_tile = bn // hd

        def body(c, carry):
            rows = pl.ds(pl.multiple_of(c * chunk, chunk), chunk)
            tiles = [gv_ref[t, rows, :] for t in range(n_tiles)]
            mu = sum(jnp.sum(x, axis=-1, keepdims=True) for x in tiles) / d_a
            cent = [x - mu for x in tiles]
            var = sum(jnp.sum(x * x, axis=-1, keepdims=True) for x in cent) / d_a
            rstd = lax.rsqrt(var + EPS)
            for t in range(n_tiles):
                cols = slice(t * bn, (t + 1) * bn)
                vn = (cent[t] * rstd * lng_ref[:, cols] + lnb_ref[:, cols]).astype(BF16)
                for hh in range(heads_per_tile):
                    head = t * heads_per_tile + hh
                    mixed = _dot(ws_ref[head], vn[:, hh * hd:(hh + 1) * hd]) + bs_ref[head]
                    o_ref[rows, head * hd:(head + 1) * hd] = mixed.astype(o_ref.dtype)
            return carry

        lax.fori_loop(0, n_chunks, body, 0)


def _spatial(h, rstd, w_in, layer, col0, ln_g, ln_b, w_s, b_s_full, bm, bn):
    t, d = h.shape
    _, heads, chunk, _ = w_s.shape
    d_a = ln_g.shape[0]
    hd = d_a // heads
    n_tiles = d_a // bn
    blk0 = col0 // bn
    kern = functools.partial(_spatial_kernel, n_tiles=n_tiles, bn=bn, chunk=chunk, hd=hd, d_a=d_a)
    return pl.pallas_call(
        kern,
        grid=(t // bm, n_tiles),
        in_specs=[pl.BlockSpec((bm, d), lambda i, j: (i, 0)),
                  pl.BlockSpec((bm, LANES), lambda i, j: (i, 0)),
                  pl.BlockSpec((None, d, bn), lambda i, j: (layer, 0, blk0 + j)),
                  pl.BlockSpec((1, d_a), lambda i, j: (0, 0)),
                  pl.BlockSpec((1, d_a), lambda i, j: (0, 0)),
                  pl.BlockSpec((None, heads, chunk, chunk), lambda i, j: (layer, 0, 0, 0)),
                  pl.BlockSpec((None, heads, chunk, hd), lambda i, j: (layer, 0, 0, 0))],
        out_specs=pl.BlockSpec((bm, d_a), lambda i, j: (i, 0)),
        out_shape=jax.ShapeDtypeStruct((t, d_a), BF16),
        scratch_shapes=[pltpu.VMEM((n_tiles, bm, bn), F32)],
        compiler_params=_params("parallel", "arbitrary"),
        name="inproj_spatial",
    )(h, rstd, w_in, ln_g.reshape(1, d_a), ln_b.reshape(1, d_a), w_s, b_s_full)


def _gate_a_kernel(h_ref, r_ref, wu_ref, wz_ref, mixed_ref, o_ref):
    for rows in _row_blocks(h_ref.shape[0]):
        h, r = h_ref[rows, :], r_ref[rows, :]
        u = jax.nn.gelu(_rowscale(_dot(h, wu_ref[...]), r))
        z = jax.nn.silu(_rowscale(_dot(h, wz_ref[...]), r))
        o_ref[rows, :] = (u * mixed_ref[rows, :].astype(F32) * z).astype(o_ref.dtype)


def _gate_a(h, rstd, w_in, layer, col_u, col_z, mixed, bm, bn):
    t, d = h.shape
    d_a = mixed.shape[1]
    bu, bz = col_u // bn, col_z // bn
    return pl.pallas_call(
        _gate_a_kernel,
        grid=(t // bm, d_a // bn),
        in_specs=[pl.BlockSpec((bm, d), lambda i, j: (i, 0)),
                  pl.BlockSpec((bm, LANES), lambda i, j: (i, 0)),
                  pl.BlockSpec((None, d, bn), lambda i, j: (layer, 0, bu + j)),
                  pl.BlockSpec((None, d, bn), lambda i, j: (layer, 0, bz + j)),
                  pl.BlockSpec((bm, bn), lambda i, j: (i, j))],
        out_specs=pl.BlockSpec((bm, bn), lambda i, j: (i, j)),
        out_shape=jax.ShapeDtypeStruct((t, d_a), BF16),
        compiler_params=_params("parallel", "arbitrary"),
        name="inproj_gate_a",
    )(h, rstd, w_in, w_in, mixed)


def _chan_dft_kernel(h_ref, r_ref, wx_ref, wz_ref, cs_ref, yce_ref, yco_ref, yse_ref, yso_ref, sz_ref, y_scr, *, gd):
    for rows in _row_blocks(h_ref.shape[0]):
        h, r = h_ref[rows, :], r_ref[rows, :]
        xb = _rowscale(_dot(h, wx_ref[...]), r).astype(BF16)
        n_rows = xb.shape[0]
        half_rows = slice(rows.start // 2, rows.start // 2 + n_rows // 2)
        lane_tiles = gd // LANES
        for g in range(xb.shape[1] // gd):
            y = _dot(xb[:, g * gd:(g + 1) * gd], cs_ref[...])
            for q in range(2 * lane_tiles):
                slot = 2 * g * lane_tiles + q
                y_scr[slot, rows, :] = y[:, q * LANES:(q + 1) * LANES]
                even = y_scr[slot, pl.ds(rows.start, n_rows // 2, stride=2), :]
                odd = y_scr[slot, pl.ds(rows.start + 1, n_rows // 2, stride=2), :]
                e_ref, o_ref = (yce_ref, yco_ref) if q < lane_tiles else (yse_ref, yso_ref)
                cols = slice(g * gd + (q % lane_tiles) * LANES, g * gd + (q % lane_tiles + 1) * LANES)
                e_ref[half_rows, cols] = even.astype(e_ref.dtype)
                o_ref[half_rows, cols] = odd.astype(o_ref.dtype)
        sz_ref[rows, :] = jax.nn.silu(_rowscale(_dot(h, wz_ref[...]), r)).astype(sz_ref.dtype)


def _chan_dft(h, rstd, w_in, layer, col_x, col_z, d_b, cs_mat, bm, bn):
    t, d = h.shape
    gd = cs_mat.shape[0]
    bx, bz = col_x // bn, col_z // bn
    half = jax.ShapeDtypeStruct((t // 2, d_b), BF16)
    half_tile = pl.BlockSpec((bm // 2, bn), lambda i, j: (i, j))
    return pl.pallas_call(
        functools.partial(_chan_dft_kernel, gd=gd),
        grid=(t // bm, d_b // bn),
        in_specs=[pl.BlockSpec((bm, d), lambda i, j: (i, 0)),
                  pl.BlockSpec((bm, LANES), lambda i, j: (i, 0)),
                  pl.BlockSpec((None, d, bn), lambda i, j: (layer, 0, bx + j)),
                  pl.BlockSpec((None, d, bn), lambda i, j: (layer, 0, bz + j)),
                  pl.BlockSpec((gd, 2 * gd), lambda i, j: (0, 0))],
        out_specs=[half_tile, half_tile, half_tile, half_tile, pl.BlockSpec((bm, bn), lambda i, j: (i, j))],
        out_shape=[half, half, half, half, jax.ShapeDtypeStruct((t, d_b), BF16)],
        scratch_shapes=[pltpu.VMEM((2 * bn // LANES, bm, LANES), F32)],
        compiler_params=_params("parallel", "arbitrary"),
        name="inproj_chan_dft",
    )(h, rstd, w_in, w_in, cs_mat)


def _sig_kernel(h_ref, r_ref, w_ref, b_ref, o_ref):
    for rows in _row_blocks(h_ref.shape[0]):
        g = _rowscale(_dot(h_ref[rows, :], w_ref[...]), r_ref[rows, :]) + b_ref[...]
        o_ref[rows, :] = jax.nn.sigmoid(g).astype(o_ref.dtype)


def _gates(h, rstd, w_in, layer, col0, b_gate_flat, bm, bn):
    t, d = h.shape
    n = b_gate_flat.shape[1]
    blk0 = col0 // bn
    return pl.pallas_call(
        _sig_kernel,
        grid=(t // bm, n // bn),
        in_specs=[pl.BlockSpec((bm, d), lambda i, j: (i, 0)),
                  pl.BlockSpec((bm, LANES), lambda i, j: (i, 0)),
                  pl.BlockSpec((None, d, bn), lambda i, j: (layer, 0, blk0 + j)),
                  pl.BlockSpec((1, bn), lambda i, j: (0, j))],
        out_specs=pl.BlockSpec((bm, bn), lambda i, j: (i, j)),
        out_shape=jax.ShapeDtypeStruct((t, n), BF16),
        compiler_params=_params("parallel", "arbitrary"),
        name="inproj_gates",
    )(h, rstd, w_in, b_gate_flat)


def _seq_dft_kernel(ce_ref, se_ref, co_ref, so_ref, yce_ref, yse_ref, yco_ref, yso_ref, sz_ref, o_ref):
    for rows in _row_blocks(ce_ref.shape[0]):
        ev = _dot(ce_ref[rows, :], yce_ref[...]) + _dot(se_ref[rows, :], yse_ref[...])
        od = _dot(co_ref[rows, :], yco_ref[...]) + _dot(so_ref[rows, :], yso_ref[...])
        o_ref[0, rows, :] = ((ev + od) * sz_ref[0, rows, :].astype(F32)).astype(o_ref.dtype)
        o_ref[1, rows, :] = ((ev - od) * sz_ref[1, rows, :].astype(F32)).astype(o_ref.dtype)


def _seq_dft(yce, yse, yco, yso, sz, tables, seq, rt, ct):
    t, d_b = sz.shape
    nb, half = t // seq, seq // 2
    sz4 = sz.reshape(nb, 2, half, d_b)
    tab = pl.BlockSpec((rt, half), lambda b, c, r: (r, 0))
    rhs = pl.BlockSpec((half, ct), lambda b, c, r: (b, c))
    halves = pl.BlockSpec((None, 2, rt, ct), lambda b, c, r: (b, 0, r, c))
    out = pl.pallas_call(
        _seq_dft_kernel,
        grid=(nb, d_b // ct, half // rt),
        in_specs=[tab, tab, tab, tab, rhs, rhs, rhs, rhs, halves],
        out_specs=halves,
        out_shape=jax.ShapeDtypeStruct((nb, 2, half, d_b), BF16),
        compiler_params=_params("parallel", "parallel", "arbitrary"),
        name="seq_dft",
    )(*tables, yce, yse, yco, yso, sz4)
    return out.reshape(t, d_b)


def _merge_kernel(a_ref, f_ref, wa_ref, wb_ref, sa_ref, sb_ref, o_ref):
    for rows in _row_blocks(a_ref.shape[0]):
        ya = _dot(a_ref[rows, :], wa_ref[...])
        yb = _dot(f_ref[rows, :], wb_ref[...])
        m = sa_ref[rows, :].astype(F32) * ya + sb_ref[rows, :].astype(F32) * yb
        o_ref[rows, :] = m.astype(o_ref.dtype)


def _merge(a, fb, w_a, w_b, layer, sg, bm, bn):
    t, d_a = a.shape
    d_b = fb.shape[1]
    d = w_a.shape[2]
    nd = d // bn
    return pl.pallas_call(
        _merge_kernel,
        grid=(t // bm, nd),
        in_specs=[pl.BlockSpec((bm, d_a), lambda i, j: (i, 0)),
                  pl.BlockSpec((bm, d_b), lambda i, j: (i, 0)),
                  pl.BlockSpec((None, d_a, bn), lambda i, j: (layer, 0, j)),
                  pl.BlockSpec((None, d_b, bn), lambda i, j: (layer, 0, j)),
                  pl.BlockSpec((bm, bn), lambda i, j: (i, j)),
                  pl.BlockSpec((bm, bn), lambda i, j: (i, nd + j))],
        out_specs=pl.BlockSpec((bm, bn), lambda i, j: (i, j)),
        out_shape=jax.ShapeDtypeStruct((t, d), BF16),
        compiler_params=_params("parallel", "arbitrary"),
        name="branch_merge",
    )(a, fb, w_a, w_b, sg, sg)


def _out_kernel(m_ref, w_ref, x_ref, xo_ref, *norm_refs, n_tiles, d):
    j = pl.program_id(1)
    for rows in _row_blocks(m_ref.shape[0]):
        xn = x_ref[rows, :] + _dot(m_ref[rows, :], w_ref[...])
        xo_ref[rows, :] = xn
        if norm_refs:
            xb_ref, r_ref, ssq_ref = norm_refs
            xb_ref[rows, :] = xn.astype(xb_ref.dtype)
            sq = xn * xn
            ssq_ref[j, rows, :] = sum(sq[:, k:k + LANES] for k in range(0, sq.shape[1], LANES))

    if norm_refs:
        @pl.when(j == n_tiles - 1)
        def _():
            for rows in _row_blocks(m_ref.shape[0]):
                tot = sum(ssq_ref[t, rows, :] for t in range(n_tiles))
                ms = jnp.sum(tot, axis=-1, keepdims=True) / d
                r_ref[rows, :] = jnp.broadcast_to(lax.rsqrt(ms + EPS), tot.shape)


def _out_proj(m, w_out, layer, x, bm, bn, emit_norm):
    t, d = m.shape
    n_tiles = d // bn
    tile = pl.BlockSpec((bm, bn), lambda i, j: (i, j))
    out_specs = [tile]
    out_shape = [jax.ShapeDtypeStruct((t, d), F32)]
    scratch = []
    if emit_norm:
        out_specs += [tile, pl.BlockSpec((bm, LANES), lambda i, j: (i, 0))]
        out_shape += [jax.ShapeDtypeStruct((t, d), BF16), jax.ShapeDtypeStruct((t, LANES), F32)]
        scratch = [pltpu.VMEM((n_tiles, bm, LANES), F32)]
    return pl.pallas_call(
        functools.partial(_out_kernel, n_tiles=n_tiles, d=d),
        grid=(t // bm, n_tiles),
        in_specs=[pl.BlockSpec((bm, d), lambda i, j: (i, 0)),
                  pl.BlockSpec((None, d, bn), lambda i, j: (layer, 0, j)),
                  tile],
        out_specs=out_specs,
        out_shape=out_shape,
        scratch_shapes=scratch,
        compiler_params=_params("parallel", "arbitrary"),
        name="out_proj",
    )(m, w_out, x)


def _cos_sin(k, m, n, scale):
    ang = ((k * m) % n).astype(F32) * np.float32(2.0 * np.pi / n)
    return jnp.cos(ang) * np.float32(scale), jnp.sin(ang) * np.float32(-scale)


def _seq_tables(seq):
    half = seq // 2
    k = jnp.arange(half, dtype=jnp.int32)[:, None]
    m = jnp.arange(half, dtype=jnp.int32)[None, :]
    ce, se = _cos_sin(k, 2 * m, seq, seq ** -0.5)
    co, so = _cos_sin(k, 2 * m + 1, seq, seq ** -0.5)
    return tuple(a.astype(BF16) for a in (ce, se, co, so))


def _trunk(x3, weights, tables):
    (w_in, sgu_ln_g, sgu_ln_b, w_s, b_s_full, w_a, w_b, b_gate, w_out, final_g, cs_mat) = weights
    bsz, seq, d = x3.shape
    t = bsz * seq
    depth, d_a, d_b = w_a.shape[0], w_a.shape[1], w_b.shape[1]
    bm = _tile(t, ROW_TILE)
    bn = _tile(d_a, COL_TILE)
    bn2 = _tile(d_a, 2 * COL_TILE)
    rt = _tile(seq // 2, 512 if seq > 2048 else 1024)
    x = x3.reshape(t, d)
    h, rstd = _prep(x, _tile(t, 512))
    for l in range(depth):
        mixed = _spatial(h, rstd, w_in, l, d_a, sgu_ln_g[l], sgu_ln_b[l], w_s, b_s_full, bm, bn2)
        a = _gate_a(h, rstd, w_in, l, 0, 2 * d_a, mixed, bm, bn)
        yce, yco, yse, yso, sz = _chan_dft(h, rstd, w_in, l, 3 * d_a, 3 * d_a + d_b, d_b, cs_mat, bm, bn)
        sg = _gates(h, rstd, w_in, l, 3 * d_a + 2 * d_b, b_gate[l].reshape(1, 2 * d), bm, bn2)
        fb = _seq_dft(yce, yse, yco, yso, sz, tables, seq, rt, bn)
        m = _merge(a, fb, w_a, w_b, l, sg, bm, bn2)
        if l + 1 < depth:
            x, h, rstd = _out_proj(m, w_out, l, x, bm, bn, True)
        else:
            (x,) = _out_proj(m, w_out, l, x, bm, bn, False)
    return _rmsnorm(x, final_g, _tile(t, 512)).reshape(bsz, seq, d)


def kernel(x_prompt, x_sample, norm_g, w_in, sgu_ln_g, sgu_ln_b, w_spatial, b_spatial, w_a, w_b, b_gate, w_out, final_g):
    d_b = w_b.shape[1]
    gd = d_b // B_GROUPS
    hd = w_a.shape[1] // b_spatial.shape[1]
    i = jnp.arange(gd, dtype=jnp.int32)
    cc, nsc = _cos_sin(i[:, None], i[None, :], gd, gd ** -0.5)
    cs_mat = jnp.concatenate([cc, -nsc], axis=1).astype(BF16)
    b_s_full = jnp.broadcast_to(b_spatial[..., None], b_spatial.shape + (hd,))
    w_in_g = (norm_g[:, :, None] * w_in).astype(BF16)
    weights = (w_in_g, sgu_ln_g, sgu_ln_b, w_spatial.astype(BF16), b_s_full,
               w_a.astype(BF16), w_b.astype(BF16), b_gate, w_out.astype(BF16), final_g, cs_mat)
    return tuple(_trunk(x3, weights, _seq_tables(x3.shape[1])) for x3 in (x_prompt, x_sample))
```

```python
import functools

import numpy as np
import jax
import jax.numpy as jnp
from jax import lax
from jax.experimental import pallas as pl
from jax.experimental.pallas import tpu as pltpu

B_GROUPS = 8
EPS = 1e-6
VMEM_LIMIT_BYTES = 56 * 1024 * 1024
ROW_TILE = 1024
COL_TILE = 512
ROW_SUB = 256
LANES = 128
F32 = jnp.float32
BF16 = jnp.bfloat16


def _params(*sem):
    return pltpu.CompilerParams(dimension_semantics=sem, vmem_limit_bytes=VMEM_LIMIT_BYTES)


def _dot(a, b):
    return jnp.dot(a, b, preferred_element_type=F32)


def _row_blocks(n_rows):
    rs = ROW_SUB if n_rows % ROW_SUB == 0 else n_rows
    return [slice(r, r + rs) for r in range(0, n_rows, rs)]


def _tile(n, pref):
    return pref if n % pref == 0 else n


def _rowscale(p, r):
    reps = p.shape[1] // r.shape[1]
    return p * (jnp.concatenate([r] * reps, axis=1) if reps > 1 else r)


def _prep_kernel(x_ref, xb_ref, r_ref):
    x = x_ref[...]
    xb_ref[...] = x.astype(xb_ref.dtype)
    ms = jnp.mean(x * x, axis=-1, keepdims=True)
    r_ref[...] = jnp.broadcast_to(lax.rsqrt(ms + EPS), r_ref.shape)


def _prep(x, bm):
    t, d = x.shape
    return pl.pallas_call(
        _prep_kernel,
        grid=(t // bm,),
        in_specs=[pl.BlockSpec((bm, d), lambda i: (i, 0))],
        out_specs=[pl.BlockSpec((bm, d), lambda i: (i, 0)), pl.BlockSpec((bm, LANES), lambda i: (i, 0))],
        out_shape=[jax.ShapeDtypeStruct((t, d), BF16), jax.ShapeDtypeStruct((t, LANES), F32)],
        compiler_params=_params("parallel"),
        name="prenorm_stats",
    )(x)


def _spatial_kernel(h_ref, r_ref, w_ref, lng_ref, lnb_ref, ws_ref, bs_ref, o_ref, gv_ref, *,
                    n_tiles, bn, chunk, hd, d_a):
    j = pl.program_id(1)
    for rows in _row_blocks(h_ref.shape[0]):
        gv_ref[j, rows, :] = jax.nn.gelu(_rowscale(_dot(h_ref[rows, :], w_ref[...]), r_ref[rows, :]))

    @pl.when(j == n_tiles - 1)
    def _():
        n_chunks = gv_ref.shape[1] // chunk
        heads_per_tile = bn // hd

        def body(c, carry):
            rows = pl.ds(pl.multiple_of(c * chunk, chunk), chunk)
            tiles = [gv_ref[t, rows, :] for t in range(n_tiles)]
            mu = sum(jnp.sum(x, axis=-1, keepdims=True) for x in tiles) / d_a
            cent = [x - mu for x in tiles]
            var = sum(jnp.sum(x * x, axis=-1, keepdims=True) for x in cent) / d_a
            rstd = lax.rsqrt(var + EPS)
            for t in range(n_tiles):
                cols = slice(t * bn, (t + 1) * bn)
                vn = (cent[t] * rstd * lng_ref[:, cols] + lnb_ref[:, cols]).astype(BF16)
                for hh in range(heads_per_tile):
                    head = t * heads_per_tile + hh
                    mixed = _dot(ws_ref[head], vn[:, hh * hd:(hh + 1) * hd]) + bs_ref[head]
                    o_ref[rows, head * hd:(head + 1) * hd] = mixed.astype(o_ref.dtype)
            return carry

        lax.fori_loop(0, n_chunks, body, 0)


def _spatial(h, rstd, w_in, layer, col0, ln_g, ln_b, w_s, b_s_full, bm, bn):
    t, d = h.shape
    _, heads, chunk, _ = w_s.shape
    d_a = ln_g.shape[0]
    hd = d_a // heads
    n_tiles = d_a // bn
    blk0 = col0 // bn
    kern = functools.partial(_spatial_kernel, n_tiles=n_tiles, bn=bn, chunk=chunk, hd=hd, d_a=d_a)
    return pl.pallas_call(
        kern,
        grid=(t // bm, n_tiles),
        in_specs=[pl.BlockSpec((bm, d), lambda i, j: (i, 0)),
                  pl.BlockSpec((bm, LANES), lambda i, j: (i, 0)),
                  pl.BlockSpec((None, d, bn), lambda i, j: (layer, 0, blk0 + j)),
                  pl.BlockSpec((1, d_a), lambda i, j: (0, 0)),
                  pl.BlockSpec((1, d_a), lambda i, j: (0, 0)),
                  pl.BlockSpec((None, heads, chunk, chunk), lambda i, j: (layer, 0, 0, 0)),
                  pl.BlockSpec((None, heads, chunk, hd), lambda i, j: (layer, 0, 0, 0))],
        out_specs=pl.BlockSpec((bm, d_a), lambda i, j: (i, 0)),
        out_shape=jax.ShapeDtypeStruct((t, d_a), BF16),
        scratch_shapes=[pltpu.VMEM((n_tiles, bm, bn), F32)],
        compiler_params=_params("parallel", "arbitrary"),
        name="inproj_spatial",
    )(h, rstd, w_in, ln_g.reshape(1, d_a), ln_b.reshape(1, d_a), w_s, b_s_full)


def _gate_a_kernel(h_ref, r_ref, wu_ref, wz_ref, mixed_ref, o_ref):
    for rows in _row_blocks(h_ref.shape[0]):
        h, r = h_ref[rows, :], r_ref[rows, :]
        u = jax.nn.gelu(_rowscale(_dot(h, wu_ref[...]), r))
        z = jax.nn.silu(_rowscale(_dot(h, wz_ref[...]), r))
        o_ref[rows, :] = (u * mixed_ref[rows, :].astype(F32) * z).astype(o_ref.dtype)


def _gate_a(h, rstd, w_in, layer, col_u, col_z, mixed, bm, bn):
    t, d = h.shape
    d_a = mixed.shape[1]
    bu, bz = col_u // bn, col_z // bn
    return pl.pallas_call(
        _gate_a_kernel,
        grid=(t // bm, d_a // bn),
        in_specs=[pl.BlockSpec((bm, d), lambda i, j: (i, 0)),
                  pl.BlockSpec((bm, LANES), lambda i, j: (i, 0)),
                  pl.BlockSpec((None, d, bn), lambda i, j: (layer, 0, bu + j)),
                  pl.BlockSpec((None, d, bn), lambda i, j: (layer, 0, bz + j)),
                  pl.BlockSpec((bm, bn), lambda i, j: (i, j))],
        out_specs=pl.BlockSpec((bm, bn), lambda i, j: (i, j)),
        out_shape=jax.ShapeDtypeStruct((t, d_a), BF16),
        compiler_params=_params("parallel", "arbitrary"),
        name="inproj_gate_a",
    )(h, rstd, w_in, w_in, mixed)


def _chan_dft_kernel(h_ref, r_ref, wx_ref, wz_ref, cs_ref, yce_ref, yco_ref, yse_ref, yso_ref, sz_ref, y_scr, *, gd):
    for rows in _row_blocks(h_ref.shape[0]):
        h, r = h_ref[rows, :], r_ref[rows, :]
        xb = _rowscale(_dot(h, wx_ref[...]), r).astype(BF16)
        n_rows = xb.shape[0]
        half_rows = slice(rows.start // 2, rows.start // 2 + n_rows // 2)
        lane_tiles = gd // LANES
        for g in range(xb.shape[1] // gd):
            y = _dot(xb[:, g * gd:(g + 1) * gd], cs_ref[...])
            for q in range(2 * lane_tiles):
                slot = 2 * g * lane_tiles + q
                y_scr[slot, rows, :] = y[:, q * LANES:(q + 1) * LANES]
                even = y_scr[slot, pl.ds(rows.start, n_rows // 2, stride=2), :]
                odd = y_scr[slot, pl.ds(rows.start + 1, n_rows // 2, stride=2), :]
                e_ref, o_ref = (yce_ref, yco_ref) if q < lane_tiles else (yse_ref, yso_ref)
                cols = slice(g * gd + (q % lane_tiles) * LANES, g * gd + (q % lane_tiles + 1) * LANES)
                e_ref[half_rows, cols] = even.astype(e_ref.dtype)
                o_ref[half_rows, cols] = odd.astype(o_ref.dtype)
        sz_ref[rows, :] = jax.nn.silu(_rowscale(_dot(h, wz_ref[...]), r)).astype(sz_ref.dtype)


def _chan_dft(h, rstd, w_in, layer, col_x, col_z, d_b, cs_mat, bm, bn):
    t, d = h.shape
    gd = cs_mat.shape[0]
    bx, bz = col_x // bn, col_z // bn
    half = jax.ShapeDtypeStruct((t // 2, d_b), BF16)
    half_tile = pl.BlockSpec((bm // 2, bn), lambda i, j: (i, j))
    return pl.pallas_call(
        functools.partial(_chan_dft_kernel, gd=gd),
        grid=(t // bm, d_b // bn),
        in_specs=[pl.BlockSpec((bm, d), lambda i, j: (i, 0)),
                  pl.BlockSpec((bm, LANES), lambda i, j: (i, 0)),
                  pl.BlockSpec((None, d, bn), lambda i, j: (layer, 0, bx + j)),
                  pl.BlockSpec((None, d, bn), lambda i, j: (layer, 0, bz + j)),
                  pl.BlockSpec((gd, 2 * gd), lambda i, j: (0, 0))],
        out_specs=[half_tile, half_tile, half_tile, half_tile, pl.BlockSpec((bm, bn), lambda i, j: (i, j))],
        out_shape=[half, half, half, half, jax.ShapeDtypeStruct((t, d_b), BF16)],
        scratch_shapes=[pltpu.VMEM((2 * bn // LANES, bm, LANES), F32)],
        compiler_params=_params("parallel", "arbitrary"),
        name="inproj_chan_dft",
    )(h, rstd, w_in, w_in, cs_mat)


def _sig_kernel(h_ref, r_ref, w_ref, b_ref, o_ref):
    for rows in _row_blocks(h_ref.shape[0]):
        g = _rowscale(_dot(h_ref[rows, :], w_ref[...]), r_ref[rows, :]) + b_ref[...]
        o_ref[rows, :] = jax.nn.sigmoid(g).astype(o_ref.dtype)


def _gates(h, rstd, w_in, layer, col0, b_gate_flat, bm, bn):
    t, d = h.shape
    n = b_gate_flat.shape[1]
    blk0 = col0 // bn
    return pl.pallas_call(
        _sig_kernel,
        grid=(t // bm, n // bn),
        in_specs=[pl.BlockSpec((bm, d), lambda i, j: (i, 0)),
                  pl.BlockSpec((bm, LANES), lambda i, j: (i, 0)),
                  pl.BlockSpec((None, d, bn), lambda i, j: (layer, 0, blk0 + j)),
                  pl.BlockSpec((1, bn), lambda i, j: (0, j))],
        out_specs=pl.BlockSpec((bm, bn), lambda i, j: (i, j)),
        out_shape=jax.ShapeDtypeStruct((t, n), BF16),
        compiler_params=_params("parallel", "arbitrary"),
        name="inproj_gates",
    )(h, rstd, w_in, b_gate_flat)


def _seq_dft_kernel(ce_ref, se_ref, co_ref, so_ref, yce_ref, yse_ref, yco_ref, yso_ref, sz_ref, o_ref):
    for rows in _row_blocks(ce_ref.shape[0]):
        ev = _dot(ce_ref[rows, :], yce_ref[...]) + _dot(se_ref[rows, :], yse_ref[...])
        od = _dot(co_ref[rows, :], yco_ref[...]) + _dot(so_ref[rows, :], yso_ref[...])
        o_ref[0, rows, :] = ((ev + od) * sz_ref[0, rows, :].astype(F32)).astype(o_ref.dtype)
        o_ref[1, rows, :] = ((ev - od) * sz_ref[1, rows, :].astype(F32)).astype(o_ref.dtype)


def _seq_dft(yce, yse, yco, yso, sz, tables, seq, rt, ct):
    t, d_b = sz.shape
    nb, half = t // seq, seq // 2
    sz4 = sz.reshape(nb, 2, half, d_b)
    tab = pl.BlockSpec((rt, half), lambda b, c, r: (r, 0))
    rhs = pl.BlockSpec((half, ct), lambda b, c, r: (b, c))
    halves = pl.BlockSpec((None, 2, rt, ct), lambda b, c, r: (b, 0, r, c))
    out = pl.pallas_call(
        _seq_dft_kernel,
        grid=(nb, d_b // ct, half // rt),
        in_specs=[tab, tab, tab, tab, rhs, rhs, rhs, rhs, halves],
        out_specs=halves,
        out_shape=jax.ShapeDtypeStruct((nb, 2, half, d_b), BF16),
        compiler_params=_params("parallel", "parallel", "arbitrary"),
        name="seq_dft",
    )(*tables, yce, yse, yco, yso, sz4)
    return out.reshape(t, d_b)


def _merge_kernel(a_ref, f_ref, wa_ref, wb_ref, sa_ref, sb_ref, o_ref):
    for rows in _row_blocks(a_ref.shape[0]):
        ya = _dot(a_ref[rows, :], wa_ref[...])
        yb = _dot(f_ref[rows, :], wb_ref[...])
        m = sa_ref[rows, :].astype(F32) * ya + sb_ref[rows, :].astype(F32) * yb
        o_ref[rows, :] = m.astype(o_ref.dtype)


def _merge(a, fb, w_a, w_b, layer, sg, bm, bn):
    t, d_a = a.shape
    d_b = fb.shape[1]
    d = w_a.shape[2]
    nd = d // bn
    return pl.pallas_call(
        _merge_kernel,
        grid=(t // bm, nd),
        in_specs=[pl.BlockSpec((bm, d_a), lambda i, j: (i, 0)),
                  pl.BlockSpec((bm, d_b), lambda i, j: (i, 0)),
                  pl.BlockSpec((None, d_a, bn), lambda i, j: (layer, 0, j)),
                  pl.BlockSpec((None, d_b, bn), lambda i, j: (layer, 0, j)),
                  pl.BlockSpec((bm, bn), lambda i, j: (i, j)),
                  pl.BlockSpec((bm, bn), lambda i, j: (i, nd + j))],
        out_specs=pl.BlockSpec((bm, bn), lambda i, j: (i, j)),
        out_shape=jax.ShapeDtypeStruct((t, d), BF16),
        compiler_params=_params("parallel", "arbitrary"),
        name="branch_merge",
    )(a, fb, w_a, w_b, sg, sg)


def _out_kernel(m_ref, w_ref, x_ref, xo_ref, xb_ref, r_ref, ssq_ref, *, n_tiles, d):
    j = pl.program_id(1)
    for rows in _row_blocks(m_ref.shape[0]):
        xn = x_ref[rows, :] + _dot(m_ref[rows, :], w_ref[...])
        xo_ref[rows, :] = xn
        xb_ref[rows, :] = xn.astype(xb_ref.dtype)
        sq = xn * xn
        ssq_ref[j, rows, :] = sum(sq[:, k:k + LANES] for k in range(0, sq.shape[1], LANES))

    @pl.when(j == n_tiles - 1)
    def _():
        for rows in _row_blocks(m_ref.shape[0]):
            tot = sum(ssq_ref[t, rows, :] for t in range(n_tiles))
            ms = jnp.sum(tot, axis=-1, keepdims=True) / d
            r_ref[rows, :] = jnp.broadcast_to(lax.rsqrt(ms + EPS), tot.shape)


def _out_proj(m, w_out, layer, x, bm, bn):
    t, d = m.shape
    n_tiles = d // bn
    tile = pl.BlockSpec((bm, bn), lambda i, j: (i, j))
    return pl.pallas_call(
        functools.partial(_out_kernel, n_tiles=n_tiles, d=d),
        grid=(t // bm, n_tiles),
        in_specs=[pl.BlockSpec((bm, d), lambda i, j: (i, 0)),
                  pl.BlockSpec((None, d, bn), lambda i, j: (layer, 0, j)),
                  tile],
        out_specs=[tile, tile, pl.BlockSpec((bm, LANES), lambda i, j: (i, 0))],
        out_shape=[jax.ShapeDtypeStruct((t, d), F32), jax.ShapeDtypeStruct((t, d), BF16),
                   jax.ShapeDtypeStruct((t, LANES), F32)],
        scratch_shapes=[pltpu.VMEM((n_tiles, bm, LANES), F32)],
        compiler_params=_params("parallel", "arbitrary"),
        name="out_proj",
    )(m, w_out, x)


def _out_final_kernel(m_ref, w_ref, x_ref, g_ref, y_ref, *, bn, d):
    m = m_ref[...]
    ssq = jnp.zeros((m.shape[0], LANES), F32)
    col_tiles = [slice(c, c + bn) for c in range(0, d, bn)]
    for cols in col_tiles:
        xn = x_ref[:, cols] + _dot(m, w_ref[:, cols])
        y_ref[:, cols] = xn
        sq = xn * xn
        ssq = ssq + sum(sq[:, k:k + LANES] for k in range(0, bn, LANES))
    scale = lax.rsqrt(jnp.sum(ssq, axis=-1, keepdims=True) / d + EPS)
    for cols in col_tiles:
        y_ref[:, cols] = y_ref[:, cols] * scale * g_ref[:, cols]


def _out_proj_final(m, w_out, layer, x, g, bm, bn):
    t, d = m.shape
    rows_full = pl.BlockSpec((bm, d), lambda i: (i, 0))
    vmem = d * d * 2 + 2 * bm * d * (2 + 4 + 4) + bm * d * 2 + 8 * bm * bn * 4
    return pl.pallas_call(
        functools.partial(_out_final_kernel, bn=bn, d=d),
        grid=(t // bm,),
        in_specs=[rows_full,
                  pl.BlockSpec((None, d, d), lambda i: (layer, 0, 0), pipeline_mode=pl.Buffered(1)),
                  rows_full,
                  pl.BlockSpec((1, d), lambda i: (0, 0))],
        out_specs=rows_full,
        out_shape=jax.ShapeDtypeStruct((t, d), F32),
        compiler_params=pltpu.CompilerParams(dimension_semantics=("parallel",), vmem_limit_bytes=vmem),
        name="out_proj_final_norm",
    )(m, w_out, x, g.reshape(1, d))


def _cos_sin(k, m, n, scale):
    ang = ((k * m) % n).astype(F32) * np.float32(2.0 * np.pi / n)
    return jnp.cos(ang) * np.float32(scale), jnp.sin(ang) * np.float32(-scale)


def _seq_tables(seq):
    half = seq // 2
    k = jnp.arange(half, dtype=jnp.int32)[:, None]
    m = jnp.arange(half, dtype=jnp.int32)[None, :]
    ce, se = _cos_sin(k, 2 * m, seq, seq ** -0.5)
    cphi, nsphi = _cos_sin(k, 1, seq, 1.0)
    co = ce * cphi - se * nsphi
    so = se * cphi + ce * nsphi
    return tuple(a.astype(BF16) for a in (ce, se, co, so))


def _trunk(x3, weights, tables):
    (w_in, sgu_ln_g, sgu_ln_b, w_s, b_s_full, w_a, w_b, b_gate, w_out, final_g, cs_mat) = weights
    bsz, seq, d = x3.shape
    t = bsz * seq
    depth, d_a, d_b = w_a.shape[0], w_a.shape[1], w_b.shape[1]
    bm = _tile(t, ROW_TILE)
    bn = _tile(d_a, COL_TILE)
    bn2 = _tile(d_a, 2 * COL_TILE)
    rt = _tile(seq // 2, 512 if seq > 2048 else 1024)
    x = x3.reshape(t, d)
    h, rstd = _prep(x, _tile(t, 512))
    for l in range(depth):
        mixed = _spatial(h, rstd, w_in, l, d_a, sgu_ln_g[l], sgu_ln_b[l], w_s, b_s_full, bm, bn2)
        a = _gate_a(h, rstd, w_in, l, 0, 2 * d_a, mixed, bm, bn)
        yce, yco, yse, yso, sz = _chan_dft(h, rstd, w_in, l, 3 * d_a, 3 * d_a + d_b, d_b, cs_mat, bm, bn)
        sg = _gates(h, rstd, w_in, l, 3 * d_a + 2 * d_b, b_gate[l].reshape(1, 2 * d), bm, bn2)
        fb = _seq_dft(yce, yse, yco, yso, sz, tables, seq, rt, bn)
        m = _merge(a, fb, w_a, w_b, l, sg, bm, bn2)
        if l + 1 < depth:
            x, h, rstd = _out_proj(m, w_out, l, x, bm, bn)
        else:
            y = _out_proj_final(m, w_out, l, x, final_g, _tile(t, ROW_SUB), bn)
    return y.reshape(bsz, seq, d)


def kernel(x_prompt, x_sample, norm_g, w_in, sgu_ln_g, sgu_ln_b, w_spatial, b_spatial, w_a, w_b, b_gate, w_out, final_g):
    d_b = w_b.shape[1]
    gd = d_b // B_GROUPS
    hd = w_a.shape[1] // b_spatial.shape[1]
    i = jnp.arange(gd, dtype=jnp.int32)
    cc, nsc = _cos_sin(i[:, None], i[None, :], gd, gd ** -0.5)
    cs_mat = jnp.concatenate([cc, -nsc], axis=1).astype(BF16)
    b_s_full = jnp.broadcast_to(b_spatial[..., None], b_spatial.shape + (hd,))
    w_in_g = (norm_g[:, :, None] * w_in).astype(BF16)
    weights = (w_in_g, sgu_ln_g, sgu_ln_b, w_spatial.astype(BF16), b_s_full,
               w_a.astype(BF16), w_b.astype(BF16), b_gate, w_out.astype(BF16), final_g, cs_mat)
    return tuple(_trunk(x3, weights, _seq_tables(x3.shape[1])) for x3 in (x_prompt, x_sample))
```

```python
import functools

import numpy as np
import jax
import jax.numpy as jnp
from jax import lax
from jax.experimental import pallas as pl
from jax.experimental.pallas import tpu as pltpu

B_GROUPS = 8
EPS = 1e-6
VMEM_LIMIT_BYTES = 56 * 1024 * 1024
ROW_TILE = 1024
COL_TILE = 512
ROW_SUB = 256
LANES = 128
F32 = jnp.float32
BF16 = jnp.bfloat16


def _params(*sem):
    return pltpu.CompilerParams(dimension_semantics=sem, vmem_limit_bytes=VMEM_LIMIT_BYTES)


def _dot(a, b):
    return jnp.dot(a, b, preferred_element_type=F32)


def _row_blocks(n_rows):
    rs = ROW_SUB if n_rows % ROW_SUB == 0 else n_rows
    return [slice(r, r + rs) for r in range(0, n_rows, rs)]


def _tile(n, pref):
    return pref if n % pref == 0 else n


def _rowscale(p, r):
    reps = p.shape[1] // r.shape[1]
    return p * (jnp.concatenate([r] * reps, axis=1) if reps > 1 else r)


def _prep_kernel(x_ref, xb_ref, r_ref):
    x = x_ref[...]
    xb_ref[...] = x.astype(xb_ref.dtype)
    ms = jnp.mean(x * x, axis=-1, keepdims=True)
    r_ref[...] = jnp.broadcast_to(lax.rsqrt(ms + EPS), r_ref.shape)


def _prep(x, bm):
    t, d = x.shape
    return pl.pallas_call(
        _prep_kernel,
        grid=(t // bm,),
        in_specs=[pl.BlockSpec((bm, d), lambda i: (i, 0))],
        out_specs=[pl.BlockSpec((bm, d), lambda i: (i, 0)), pl.BlockSpec((bm, LANES), lambda i: (i, 0))],
        out_shape=[jax.ShapeDtypeStruct((t, d), BF16), jax.ShapeDtypeStruct((t, LANES), F32)],
        compiler_params=_params("parallel"),
        name="prenorm_stats",
    )(x)


def _spatial_kernel(h_ref, r_ref, w_ref, lng_ref, lnb_ref, ws_ref, bs_ref, o_ref, gv_ref, *,
                    n_tiles, bn, chunk, hd, d_a):
    j = pl.program_id(1)
    for rows in _row_blocks(h_ref.shape[0]):
        gv_ref[j, rows, :] = jax.nn.gelu(_rowscale(_dot(h_ref[rows, :], w_ref[...]), r_ref[rows, :]))

    @pl.when(j == n_tiles - 1)
    def _():
        n_chunks = gv_ref.shape[1] // chunk
        heads_per_tile = bn // hd

        def body(c, carry):
            rows = pl.ds(pl.multiple_of(c * chunk, chunk), chunk)
            tiles = [gv_ref[t, rows, :] for t in range(n_tiles)]
            mu = sum(jnp.sum(x, axis=-1, keepdims=True) for x in tiles) / d_a
            cent = [x - mu for x in tiles]
            var = sum(jnp.sum(x * x, axis=-1, keepdims=True) for x in cent) / d_a
            rstd = lax.rsqrt(var + EPS)
            for t in range(n_tiles):
                cols = slice(t * bn, (t + 1) * bn)
                vn = (cent[t] * rstd * lng_ref[:, cols] + lnb_ref[:, cols]).astype(BF16)
                for hh in range(heads_per_tile):
                    head = t * heads_per_tile + hh
                    mixed = _dot(ws_ref[head], vn[:, hh * hd:(hh + 1) * hd]) + bs_ref[head]
                    o_ref[rows, head * hd:(head + 1) * hd] = mixed.astype(o_ref.dtype)
            return carry

        lax.fori_loop(0, n_chunks, body, 0)


def _spatial(h, rstd, w_in, layer, col0, ln_g, ln_b, w_s, b_s_full, bm, bn):
    t, d = h.shape
    _, heads, chunk, _ = w_s.shape
    d_a = ln_g.shape[0]
    hd = d_a // heads
    n_tiles = d_a // bn
    blk0 = col0 // bn
    kern = functools.partial(_spatial_kernel, n_tiles=n_tiles, bn=bn, chunk=chunk, hd=hd, d_a=d_a)
    return pl.pallas_call(
        kern,
        grid=(t // bm, n_tiles),
        in_specs=[pl.BlockSpec((bm, d), lambda i, j: (i, 0)),
                  pl.BlockSpec((bm, LANES), lambda i, j: (i, 0)),
                  pl.BlockSpec((None, d, bn), lambda i, j: (layer, 0, blk0 + j)),
                  pl.BlockSpec((1, d_a), lambda i, j: (0, 0)),
                  pl.BlockSpec((1, d_a), lambda i, j: (0, 0)),
                  pl.BlockSpec((None, heads, chunk, chunk), lambda i, j: (layer, 0, 0, 0)),
                  pl.BlockSpec((None, heads, chunk, hd), lambda i, j: (layer, 0, 0, 0))],
        out_specs=pl.BlockSpec((bm, d_a), lambda i, j: (i, 0)),
        out_shape=jax.ShapeDtypeStruct((t, d_a), BF16),
        scratch_shapes=[pltpu.VMEM((n_tiles, bm, bn), F32)],
        compiler_params=_params("parallel", "arbitrary"),
        name="inproj_spatial",
    )(h, rstd, w_in, ln_g.reshape(1, d_a), ln_b.reshape(1, d_a), w_s, b_s_full)


def _gate_a_kernel(h_ref, r_ref, wu_ref, wz_ref, mixed_ref, o_ref):
    for rows in _row_blocks(h_ref.shape[0]):
        h, r = h_ref[rows, :], r_ref[rows, :]
        u = jax.nn.gelu(_rowscale(_dot(h, wu_ref[...]), r))
        z = jax.nn.silu(_rowscale(_dot(h, wz_ref[...]), r))
        o_ref[rows, :] = (u * mixed_ref[rows, :].astype(F32) * z).astype(o_ref.dtype)


def _gate_a(h, rstd, w_in, layer, col_u, col_z, mixed, bm, bn):
    t, d = h.shape
    d_a = mixed.shape[1]
    bu, bz = col_u // bn, col_z // bn
    return pl.pallas_call(
        _gate_a_kernel,
        grid=(t // bm, d_a // bn),
        in_specs=[pl.BlockSpec((bm, d), lambda i, j: (i, 0)),
                  pl.BlockSpec((bm, LANES), lambda i, j: (i, 0)),
                  pl.BlockSpec((None, d, bn), lambda i, j: (layer, 0, bu + j)),
                  pl.BlockSpec((None, d, bn), lambda i, j: (layer, 0, bz + j)),
                  pl.BlockSpec((bm, bn), lambda i, j: (i, j))],
        out_specs=pl.BlockSpec((bm, bn), lambda i, j: (i, j)),
        out_shape=jax.ShapeDtypeStruct((t, d_a), BF16),
        compiler_params=_params("parallel", "arbitrary"),
        name="inproj_gate_a",
    )(h, rstd, w_in, w_in, mixed)


def _fourier_in_kernel(h_ref, r_ref, wx_ref, wz_ref, xe_ref, xo_ref, sz_ref, x_scr):
    for rows in _row_blocks(h_ref.shape[0]):
        h, r = h_ref[rows, :], r_ref[rows, :]
        xb = _rowscale(_dot(h, wx_ref[...]), r)
        n_half = xb.shape[0] // 2
        half_rows = slice(rows.start // 2, rows.start // 2 + n_half)
        for q in range(xb.shape[1] // LANES):
            cols = slice(q * LANES, (q + 1) * LANES)
            x_scr[q, rows, :] = xb[:, cols]
            xe_ref[half_rows, cols] = x_scr[q, pl.ds(rows.start, n_half, stride=2), :].astype(xe_ref.dtype)
            xo_ref[half_rows, cols] = x_scr[q, pl.ds(rows.start + 1, n_half, stride=2), :].astype(xo_ref.dtype)
        sz_ref[rows, :] = jax.nn.silu(_rowscale(_dot(h, wz_ref[...]), r)).astype(sz_ref.dtype)


def _fourier_in(h, rstd, w_in, layer, col_x, col_z, d_b, bm, bn):
    t, d = h.shape
    bx, bz = col_x // bn, col_z // bn
    half = jax.ShapeDtypeStruct((t // 2, d_b), BF16)
    half_tile = pl.BlockSpec((bm // 2, bn), lambda i, j: (i, j))
    return pl.pallas_call(
        _fourier_in_kernel,
        grid=(t // bm, d_b // bn),
        in_specs=[pl.BlockSpec((bm, d), lambda i, j: (i, 0)),
                  pl.BlockSpec((bm, LANES), lambda i, j: (i, 0)),
                  pl.BlockSpec((None, d, bn), lambda i, j: (layer, 0, bx + j)),
                  pl.BlockSpec((None, d, bn), lambda i, j: (layer, 0, bz + j))],
        out_specs=[half_tile, half_tile, pl.BlockSpec((bm, bn), lambda i, j: (i, j))],
        out_shape=[half, half, jax.ShapeDtypeStruct((t, d_b), BF16)],
        scratch_shapes=[pltpu.VMEM((bn // LANES, bm, LANES), F32)],
        compiler_params=_params("parallel", "arbitrary"),
        name="inproj_fourier",
    )(h, rstd, w_in, w_in)


def _sig_kernel(h_ref, r_ref, w_ref, b_ref, o_ref):
    for rows in _row_blocks(h_ref.shape[0]):
        g = _rowscale(_dot(h_ref[rows, :], w_ref[...]), r_ref[rows, :]) + b_ref[...]
        o_ref[rows, :] = jax.nn.sigmoid(g).astype(o_ref.dtype)


def _gates(h, rstd, w_in, layer, col0, b_gate_flat, bm, bn):
    t, d = h.shape
    n = b_gate_flat.shape[1]
    blk0 = col0 // bn
    return pl.pallas_call(
        _sig_kernel,
        grid=(t // bm, n // bn),
        in_specs=[pl.BlockSpec((bm, d), lambda i, j: (i, 0)),
                  pl.BlockSpec((bm, LANES), lambda i, j: (i, 0)),
                  pl.BlockSpec((None, d, bn), lambda i, j: (layer, 0, blk0 + j)),
                  pl.BlockSpec((1, bn), lambda i, j: (0, j))],
        out_specs=pl.BlockSpec((bm, bn), lambda i, j: (i, j)),
        out_shape=jax.ShapeDtypeStruct((t, n), BF16),
        compiler_params=_params("parallel", "arbitrary"),
        name="inproj_gates",
    )(h, rstd, w_in, b_gate_flat)


def _fold_kernel(pe_ref, po_ref, xe_ref, xo_ref, see_ref, dee_ref, soo_ref, doo_ref, xh_ref):
    q = pe_ref.shape[0]
    for x_ref, p_ref, s_ref, d_ref in ((xe_ref, pe_ref, see_ref, dee_ref), (xo_ref, po_ref, soo_ref, doo_ref)):
        for rows in _row_blocks(q):
            lo = x_ref[rows, :].astype(F32)
            mirror = _dot(p_ref[rows, :], x_ref[q:, :])
            s_ref[rows, :] = (lo + mirror).astype(s_ref.dtype)
            d_ref[rows, :] = (lo - mirror).astype(d_ref.dtype)
    xh_ref[...] = jnp.broadcast_to(xe_ref[q:q + 1, :].astype(F32), xh_ref.shape)


def _fnet_kernel(tce_ref, tse_ref, tco_ref, tso_ref, cc_ref, sc_ref, see_ref, dee_ref, soo_ref, doo_ref, xh_ref,
                 sz_ref, o_ref, *, gd, scale):
    rt, ct = tce_ref.shape[0], o_ref.shape[-1]
    k_base = pl.program_id(2) * rt
    for rows in _row_blocks(rt):
        k0 = k_base + rows.start + lax.broadcasted_iota(jnp.int32, (rows.stop - rows.start, ct), 0)
        nyquist = jnp.where((k0 & 1) == 0, scale, -scale) * xh_ref[0:1, :]
        er = _dot(tce_ref[rows, :], see_ref[...]) + nyquist
        ei = _dot(tse_ref[rows, :], dee_ref[...])
        orr = _dot(tco_ref[rows, :], soo_ref[...])
        oi = _dot(tso_ref[rows, :], doo_ref[...])
        for half, (gr, gi) in enumerate(((er + orr, ei + oi), (er - orr, ei - oi))):
            gr, gi = gr.astype(BF16), gi.astype(BF16)
            for g in range(ct // gd):
                cols = slice(g * gd, (g + 1) * gd)
                y = _dot(gr[:, cols], cc_ref[...]) + _dot(gi[:, cols], sc_ref[...])
                o_ref[half, rows, cols] = (y * sz_ref[half, rows, cols].astype(F32)).astype(o_ref.dtype)


def _fourier_mix(xe, xo, sz, tables, cc, sc, seq, rt, ct):
    t, d_b = sz.shape
    nb, half, quarter = t // seq, seq // 2, seq // 4
    tce, tse, tco, tso, pe, po = tables
    gd = cc.shape[0]
    nct = d_b // ct
    perm = pl.BlockSpec((quarter, quarter), lambda b, c: (0, 0))
    parity = pl.BlockSpec((half, ct), lambda b, c: (b, c))
    folded = pl.BlockSpec((quarter, ct), lambda b, c: (b, c))
    folded_shape = jax.ShapeDtypeStruct((t // 4, d_b), BF16)
    see, dee, soo, doo, xh = pl.pallas_call(
        _fold_kernel,
        grid=(nb, nct),
        in_specs=[perm, perm, parity, parity],
        out_specs=[folded] * 4 + [pl.BlockSpec((None, 8, ct), lambda b, c: (b, 0, c))],
        out_shape=[folded_shape] * 4 + [jax.ShapeDtypeStruct((nb, 8, d_b), F32)],
        compiler_params=_params("parallel", "parallel"),
        name="fourier_fold",
    )(pe, po, xe, xo)

    sz4 = sz.reshape(nb, 2, half, d_b)
    tab = pl.BlockSpec((rt, quarter), lambda b, c, r: (r, 0))
    chan = pl.BlockSpec((gd, gd), lambda b, c, r: (0, 0))
    rhs = pl.BlockSpec((quarter, ct), lambda b, c, r: (b, c))
    halves = pl.BlockSpec((None, 2, rt, ct), lambda b, c, r: (b, 0, r, c))
    out = pl.pallas_call(
        functools.partial(_fnet_kernel, gd=gd, scale=float(seq) ** -0.5),
        grid=(nb, nct, half // rt),
        in_specs=[tab, tab, tab, tab, chan, chan, rhs, rhs, rhs, rhs,
                  pl.BlockSpec((None, 8, ct), lambda b, c, r: (b, 0, c)), halves],
        out_specs=halves,
        out_shape=jax.ShapeDtypeStruct((nb, 2, half, d_b), BF16),
        compiler_params=_params("parallel", "parallel", "arbitrary"),
        name="fourier_dft",
    )(tce, tse, tco, tso, cc, sc, see, dee, soo, doo, xh, sz4)
    return out.reshape(t, d_b)


def _merge_kernel(a_ref, f_ref, wa_ref, wb_ref, sa_ref, sb_ref, o_ref):
    for rows in _row_blocks(a_ref.shape[0]):
        ya = _dot(a_ref[rows, :], wa_ref[...])
        yb = _dot(f_ref[rows, :], wb_ref[...])
        m = sa_ref[rows, :].astype(F32) * ya + sb_ref[rows, :].astype(F32) * yb
        o_ref[rows, :] = m.astype(o_ref.dtype)


def _merge(a, fb, w_a, w_b, layer, sg, bm, bn):
    t, d_a = a.shape
    d_b = fb.shape[1]
    d = w_a.shape[2]
    nd = d // bn
    return pl.pallas_call(
        _merge_kernel,
        grid=(t // bm, nd),
        in_specs=[pl.BlockSpec((bm, d_a), lambda i, j: (i, 0)),
                  pl.BlockSpec((bm, d_b), lambda i, j: (i, 0)),
                  pl.BlockSpec((None, d_a, bn), lambda i, j: (layer, 0, j)),
                  pl.BlockSpec((None, d_b, bn), lambda i, j: (layer, 0, j)),
                  pl.BlockSpec((bm, bn), lambda i, j: (i, j)),
                  pl.BlockSpec((bm, bn), lambda i, j: (i, nd + j))],
        out_specs=pl.BlockSpec((bm, bn), lambda i, j: (i, j)),
        out_shape=jax.ShapeDtypeStruct((t, d), BF16),
        compiler_params=_params("parallel", "arbitrary"),
        name="branch_merge",
    )(a, fb, w_a, w_b, sg, sg)


def _out_kernel(m_ref, w_ref, x_ref, xo_ref, xb_ref, r_ref, ssq_ref, *, n_tiles, d):
    j = pl.program_id(1)
    for rows in _row_blocks(m_ref.shape[0]):
        xn = x_ref[rows, :] + _dot(m_ref[rows, :], w_ref[...])
        xo_ref[rows, :] = xn
        xb_ref[rows, :] = xn.astype(xb_ref.dtype)
        sq = xn * xn
        ssq_ref[j, rows, :] = sum(sq[:, k:k + LANES] for k in range(0, sq.shape[1], LANES))

    @pl.when(j == n_tiles - 1)
    def _():
        for rows in _row_blocks(m_ref.shape[0]):
            tot = sum(ssq_ref[t, rows, :] for t in range(n_tiles))
            ms = jnp.sum(tot, axis=-1, keepdims=True) / d
            r_ref[rows, :] = jnp.broadcast_to(lax.rsqrt(ms + EPS), tot.shape)


def _out_proj(m, w_out, layer, x, bm, bn):
    t, d = m.shape
    n_tiles = d // bn
    tile = pl.BlockSpec((bm, bn), lambda i, j: (i, j))
    return pl.pallas_call(
        functools.partial(_out_kernel, n_tiles=n_tiles, d=d),
        grid=(t // bm, n_tiles),
        in_specs=[pl.BlockSpec((bm, d), lambda i, j: (i, 0)),
                  pl.BlockSpec((None, d, bn), lambda i, j: (layer, 0, j)),
                  tile],
        out_specs=[tile, tile, pl.BlockSpec((bm, LANES), lambda i, j: (i, 0))],
        out_shape=[jax.ShapeDtypeStruct((t, d), F32), jax.ShapeDtypeStruct((t, d), BF16),
                   jax.ShapeDtypeStruct((t, LANES), F32)],
        scratch_shapes=[pltpu.VMEM((n_tiles, bm, LANES), F32)],
        compiler_params=_params("parallel", "arbitrary"),
        name="out_proj",
    )(m, w_out, x)


def _out_final_kernel(m_ref, w_ref, x_ref, g_ref, y_ref, *, bn, d):
    m = m_ref[...]
    ssq = jnp.zeros((m.shape[0], LANES), F32)
    col_tiles = [slice(c, c + bn) for c in range(0, d, bn)]
    for cols in col_tiles:
        xn = x_ref[:, cols] + _dot(m, w_ref[:, cols])
        y_ref[:, cols] = xn
        sq = xn * xn
        ssq = ssq + sum(sq[:, k:k + LANES] for k in range(0, bn, LANES))
    scale = lax.rsqrt(jnp.sum(ssq, axis=-1, keepdims=True) / d + EPS)
    for cols in col_tiles:
        y_ref[:, cols] = y_ref[:, cols] * scale * g_ref[:, cols]


def _out_proj_final(m, w_out, layer, x, g, bm, bn):
    t, d = m.shape
    rows_full = pl.BlockSpec((bm, d), lambda i: (i, 0))
    vmem = d * d * 2 + 2 * bm * d * (2 + 4 + 4) + bm * d * 2 + 8 * bm * bn * 4
    return pl.pallas_call(
        functools.partial(_out_final_kernel, bn=bn, d=d),
        grid=(t // bm,),
        in_specs=[rows_full,
                  pl.BlockSpec((None, d, d), lambda i: (layer, 0, 0), pipeline_mode=pl.Buffered(1)),
                  rows_full,
                  pl.BlockSpec((1, d), lambda i: (0, 0))],
        out_specs=rows_full,
        out_shape=jax.ShapeDtypeStruct((t, d), F32),
        compiler_params=pltpu.CompilerParams(dimension_semantics=("parallel",), vmem_limit_bytes=vmem),
        name="out_proj_final_norm",
    )(m, w_out, x, g.reshape(1, d))


def _cos_sin(k, m, n, scale):
    ang = ((k * m) % n).astype(F32) * np.float32(2.0 * np.pi / n)
    return jnp.cos(ang) * np.float32(scale), jnp.sin(ang) * np.float32(-scale)


def _seq_tables(seq):
    half, quarter = seq // 2, seq // 4
    k = jnp.arange(half, dtype=jnp.int32)[:, None]
    m = jnp.arange(quarter, dtype=jnp.int32)[None, :]
    ce, se = _cos_sin(k, 2 * m, seq, seq ** -0.5)
    cphi, nsphi = _cos_sin(k, 1, seq, 1.0)
    co = ce * cphi - se * nsphi
    so = se * cphi + ce * nsphi
    i = jnp.arange(quarter, dtype=jnp.int32)
    ij = i[:, None] + i[None, :]
    mirror_even = ij == quarter
    mirror_odd = ij == quarter - 1
    return tuple(a.astype(BF16) for a in (ce, se, co, so, mirror_even, mirror_odd))


def _trunk(x3, weights, tables):
    (w_in, sgu_ln_g, sgu_ln_b, w_s, b_s_full, w_a, w_b, b_gate, w_out, final_g, cc, sc) = weights
    bsz, seq, d = x3.shape
    t = bsz * seq
    depth, d_a, d_b = w_a.shape[0], w_a.shape[1], w_b.shape[1]
    bm = _tile(t, ROW_TILE)
    bn = _tile(d_a, COL_TILE)
    bn2 = _tile(d_a, 2 * COL_TILE)
    rt = _tile(seq // 2, 512 if seq > 2048 else 1024)
    x = x3.reshape(t, d)
    h, rstd = _prep(x, _tile(t, 512))
    for l in range(depth):
        mixed = _spatial(h, rstd, w_in, l, d_a, sgu_ln_g[l], sgu_ln_b[l], w_s, b_s_full, bm, bn2)
        a = _gate_a(h, rstd, w_in, l, 0, 2 * d_a, mixed, bm, bn)
        xe, xo, sz = _fourier_in(h, rstd, w_in, l, 3 * d_a, 3 * d_a + d_b, d_b, bm, bn)
        sg = _gates(h, rstd, w_in, l, 3 * d_a + 2 * d_b, b_gate[l].reshape(1, 2 * d), bm, bn2)
        fb = _fourier_mix(xe, xo, sz, tables, cc, sc, seq, rt, bn)
        m = _merge(a, fb, w_a, w_b, l, sg, bm, bn2)
        if l + 1 < depth:
            x, h, rstd = _out_proj(m, w_out, l, x, bm, bn)
        else:
            y = _out_proj_final(m, w_out, l, x, final_g, _tile(t, ROW_SUB), bn)
    return y.reshape(bsz, seq, d)


def kernel(x_prompt, x_sample, norm_g, w_in, sgu_ln_g, sgu_ln_b, w_spatial, b_spatial, w_a, w_b, b_gate, w_out, final_g):
    d_b = w_b.shape[1]
    gd = d_b // B_GROUPS
    hd = w_a.shape[1] // b_spatial.shape[1]
    i = jnp.arange(gd, dtype=jnp.int32)
    cc, nsc = _cos_sin(i[:, None], i[None, :], gd, gd ** -0.5)
    b_s_full = jnp.broadcast_to(b_spatial[..., None], b_spatial.shape + (hd,))
    w_in_g = (norm_g[:, :, None] * w_in).astype(BF16)
    weights = (w_in_g, sgu_ln_g, sgu_ln_b, w_spatial.astype(BF16), b_s_full,
               w_a.astype(BF16), w_b.astype(BF16), b_gate, w_out.astype(BF16), final_g,
               cc.astype(BF16), (-nsc).astype(BF16))
    return tuple(_trunk(x3, weights, _seq_tables(x3.shape[1])) for x3 in (x_prompt, x_sample))
```

```python
import functools

import numpy as np
import jax
import jax.numpy as jnp
from jax import lax
from jax.experimental import pallas as pl
from jax.experimental.pallas import tpu as pltpu

B_GROUPS = 8
EPS = 1e-6
VMEM_LIMIT_BYTES = 56 * 1024 * 1024
ROW_TILE = 1024
COL_TILE = 512
ROW_SUB = 256
LANES = 128
F32 = jnp.float32
BF16 = jnp.bfloat16


def _params(*sem):
    return pltpu.CompilerParams(dimension_semantics=sem, vmem_limit_bytes=VMEM_LIMIT_BYTES)


def _dot(a, b):
    return jnp.dot(a, b, preferred_element_type=F32)


def _row_blocks(n_rows):
    rs = ROW_SUB if n_rows % ROW_SUB == 0 else n_rows
    return [slice(r, r + rs) for r in range(0, n_rows, rs)]


def _tile(n, pref):
    return pref if n % pref == 0 else n


def _rowwide(r, n):
    reps = n // r.shape[1]
    return jnp.concatenate([r] * reps, axis=1) if reps > 1 else r


def _rowscale(p, r):
    return p * _rowwide(r, p.shape[1])


def _prep_kernel(x_ref, xb_ref, r_ref):
    x = x_ref[...]
    xb_ref[...] = x.astype(xb_ref.dtype)
    ms = jnp.mean(x * x, axis=-1, keepdims=True)
    r_ref[...] = jnp.broadcast_to(lax.rsqrt(ms + EPS), r_ref.shape)


def _prep(x, bm):
    t, d = x.shape
    return pl.pallas_call(
        _prep_kernel,
        grid=(t // bm,),
        in_specs=[pl.BlockSpec((bm, d), lambda i: (i, 0))],
        out_specs=[pl.BlockSpec((bm, d), lambda i: (i, 0)), pl.BlockSpec((bm, LANES), lambda i: (i, 0))],
        out_shape=[jax.ShapeDtypeStruct((t, d), BF16), jax.ShapeDtypeStruct((t, LANES), F32)],
        compiler_params=_params("parallel"),
        name="prenorm_stats",
    )(x)


def _lane_fold(x):
    return sum(x[:, k:k + LANES] for k in range(0, x.shape[1], LANES))


def _gelu_v_kernel(h_ref, r_ref, w_ref, gv_ref, mu_ref, rs_ref, s1_scr, s2_scr, *, n_tiles, d_a):
    j = pl.program_id(1)
    for rows in _row_blocks(h_ref.shape[0]):
        gv = jax.nn.gelu(_rowscale(_dot(h_ref[rows, :], w_ref[...]), r_ref[rows, :]))
        gv_ref[rows, :] = gv
        s1_scr[j, rows, :] = _lane_fold(gv)
        s2_scr[j, rows, :] = _lane_fold(gv * gv)

    @pl.when(j == n_tiles - 1)
    def _():
        for rows in _row_blocks(h_ref.shape[0]):
            s1 = jnp.sum(sum(s1_scr[t, rows, :] for t in range(n_tiles)), axis=-1, keepdims=True)
            s2 = jnp.sum(sum(s2_scr[t, rows, :] for t in range(n_tiles)), axis=-1, keepdims=True)
            mu = s1 / d_a
            var = jnp.maximum(s2 / d_a - mu * mu, 0.0)
            mu_ref[rows, :] = jnp.broadcast_to(mu, mu_ref[rows, :].shape)
            rs_ref[rows, :] = jnp.broadcast_to(lax.rsqrt(var + EPS), rs_ref[rows, :].shape)


def _gelu_v(h, rstd, w_in, layer, col0, d_a, bm, bn):
    t, d = h.shape
    n_tiles = d_a // bn
    blk0 = col0 // bn
    stat = pl.BlockSpec((bm, LANES), lambda i, j: (i, 0))
    stat_shape = jax.ShapeDtypeStruct((t, LANES), F32)
    return pl.pallas_call(
        functools.partial(_gelu_v_kernel, n_tiles=n_tiles, d_a=d_a),
        grid=(t // bm, n_tiles),
        in_specs=[pl.BlockSpec((bm, d), lambda i, j: (i, 0)),
                  stat,
                  pl.BlockSpec((None, d, bn), lambda i, j: (layer, 0, blk0 + j))],
        out_specs=[pl.BlockSpec((bm, bn), lambda i, j: (i, j)), stat, stat],
        out_shape=[jax.ShapeDtypeStruct((t, d_a), F32), stat_shape, stat_shape],
        scratch_shapes=[pltpu.VMEM((n_tiles, bm, LANES), F32), pltpu.VMEM((n_tiles, bm, LANES), F32)],
        compiler_params=_params("parallel", "arbitrary"),
        name="inproj_gelu_v",
    )(h, rstd, w_in)


def _gate_a_kernel(h_ref, r_ref, wu_ref, wz_ref, gv_ref, mu_ref, rs_ref, lng_ref, lnb_ref, ws_ref, bs_ref, o_ref, *,
                   chunk, hd):
    for rows in _row_blocks(h_ref.shape[0]):
        h, r = h_ref[rows, :], r_ref[rows, :]
        u = jax.nn.gelu(_rowscale(_dot(h, wu_ref[...]), r))
        z = jax.nn.silu(_rowscale(_dot(h, wz_ref[...]), r))
        vn = _rowscale(gv_ref[rows, :] - _rowwide(mu_ref[rows, :], u.shape[1]), rs_ref[rows, :])
        vn = (vn * lng_ref[...] + lnb_ref[...]).astype(BF16)
        n_rows = vn.shape[0]
        sub = chunk if n_rows % chunk == 0 else n_rows
        mixed = jnp.concatenate(
            [jnp.concatenate(
                [_dot(ws_ref[hh], vn[c:c + sub, hh * hd:(hh + 1) * hd]) + bs_ref[hh] for hh in range(vn.shape[1] // hd)],
                axis=1)
             for c in range(0, n_rows, sub)], axis=0)
        o_ref[rows, :] = (u * mixed * z).astype(o_ref.dtype)


def _gate_a(h, rstd, w_in, layer, col_u, col_z, gv, mu, rs, ln_g, ln_b, w_s, b_s_full, bm, bn):
    t, d = h.shape
    d_a = gv.shape[1]
    _, heads, chunk, _ = w_s.shape
    hd = d_a // heads
    hpt = bn // hd
    bu, bz = col_u // bn, col_z // bn
    stat = pl.BlockSpec((bm, LANES), lambda i, j: (i, 0))
    tile = pl.BlockSpec((bm, bn), lambda i, j: (i, j))
    vec = pl.BlockSpec((1, bn), lambda i, j: (0, j))
    return pl.pallas_call(
        functools.partial(_gate_a_kernel, chunk=chunk, hd=hd),
        grid=(t // bm, d_a // bn),
        in_specs=[pl.BlockSpec((bm, d), lambda i, j: (i, 0)),
                  stat,
                  pl.BlockSpec((None, d, bn), lambda i, j: (layer, 0, bu + j)),
                  pl.BlockSpec((None, d, bn), lambda i, j: (layer, 0, bz + j)),
                  tile, stat, stat, vec, vec,
                  pl.BlockSpec((None, hpt, chunk, chunk), lambda i, j: (layer, j, 0, 0)),
                  pl.BlockSpec((None, hpt, chunk, hd), lambda i, j: (layer, j, 0, 0))],
        out_specs=tile,
        out_shape=jax.ShapeDtypeStruct((t, d_a), BF16),
        compiler_params=_params("parallel", "arbitrary"),
        name="inproj_gate_a",
    )(h, rstd, w_in, w_in, gv, mu, rs, ln_g.reshape(1, d_a), ln_b.reshape(1, d_a), w_s, b_s_full)


def _fourier_in_kernel(h_ref, r_ref, wx_ref, wz_ref, xe_ref, xo_ref, sz_ref, x_scr):
    for rows in _row_blocks(h_ref.shape[0]):
        h, r = h_ref[rows, :], r_ref[rows, :]
        xb = _rowscale(_dot(h, wx_ref[...]), r)
        n_half = xb.shape[0] // 2
        half_rows = slice(rows.start // 2, rows.start // 2 + n_half)
        for q in range(xb.shape[1] // LANES):
            cols = slice(q * LANES, (q + 1) * LANES)
            x_scr[q, rows, :] = xb[:, cols]
            xe_ref[half_rows, cols] = x_scr[q, pl.ds(rows.start, n_half, stride=2), :].astype(xe_ref.dtype)
            xo_ref[half_rows, cols] = x_scr[q, pl.ds(rows.start + 1, n_half, stride=2), :].astype(xo_ref.dtype)
        sz_ref[rows, :] = jax.nn.silu(_rowscale(_dot(h, wz_ref[...]), r)).astype(sz_ref.dtype)


def _fourier_in(h, rstd, w_in, layer, col_x, col_z, d_b, bm, bn):
    t, d = h.shape
    bx, bz = col_x // bn, col_z // bn
    half = jax.ShapeDtypeStruct((t // 2, d_b), BF16)
    half_tile = pl.BlockSpec((bm // 2, bn), lambda i, j: (i, j))
    return pl.pallas_call(
        _fourier_in_kernel,
        grid=(t // bm, d_b // bn),
        in_specs=[pl.BlockSpec((bm, d), lambda i, j: (i, 0)),
                  pl.BlockSpec((bm, LANES), lambda i, j: (i, 0)),
                  pl.BlockSpec((None, d, bn), lambda i, j: (layer, 0, bx + j)),
                  pl.BlockSpec((None, d, bn), lambda i, j: (layer, 0, bz + j))],
        out_specs=[half_tile, half_tile, pl.BlockSpec((bm, bn), lambda i, j: (i, j))],
        out_shape=[half, half, jax.ShapeDtypeStruct((t, d_b), BF16)],
        scratch_shapes=[pltpu.VMEM((bn // LANES, bm, LANES), F32)],
        compiler_params=_params("parallel", "arbitrary"),
        name="inproj_fourier",
    )(h, rstd, w_in, w_in)


def _sig_kernel(h_ref, r_ref, w_ref, b_ref, o_ref):
    for rows in _row_blocks(h_ref.shape[0]):
        g = _rowscale(_dot(h_ref[rows, :], w_ref[...]), r_ref[rows, :]) + b_ref[...]
        o_ref[rows, :] = jax.nn.sigmoid(g).astype(o_ref.dtype)


def _gates(h, rstd, w_in, layer, col0, b_gate_flat, bm, bn):
    t, d = h.shape
    n = b_gate_flat.shape[1]
    blk0 = col0 // bn
    return pl.pallas_call(
        _sig_kernel,
        grid=(t // bm, n // bn),
        in_specs=[pl.BlockSpec((bm, d), lambda i, j: (i, 0)),
                  pl.BlockSpec((bm, LANES), lambda i, j: (i, 0)),
                  pl.BlockSpec((None, d, bn), lambda i, j: (layer, 0, blk0 + j)),
                  pl.BlockSpec((1, bn), lambda i, j: (0, j))],
        out_specs=pl.BlockSpec((bm, bn), lambda i, j: (i, j)),
        out_shape=jax.ShapeDtypeStruct((t, n), BF16),
        compiler_params=_params("parallel", "arbitrary"),
        name="inproj_gates",
    )(h, rstd, w_in, b_gate_flat)


def _fold_kernel(pe_ref, po_ref, xe_ref, xo_ref, see_ref, dee_ref, soo_ref, doo_ref, xh_ref):
    q = pe_ref.shape[0]
    for x_ref, p_ref, s_ref, d_ref in ((xe_ref, pe_ref, see_ref, dee_ref), (xo_ref, po_ref, soo_ref, doo_ref)):
        for rows in _row_blocks(q):
            lo = x_ref[rows, :].astype(F32)
            mirror = _dot(p_ref[rows, :], x_ref[q:, :])
            s_ref[rows, :] = (lo + mirror).astype(s_ref.dtype)
            d_ref[rows, :] = (lo - mirror).astype(d_ref.dtype)
    xh_ref[...] = jnp.broadcast_to(xe_ref[q:q + 1, :].astype(F32), xh_ref.shape)


def _fnet_kernel(tce_ref, tse_ref, tco_ref, tso_ref, cc_ref, sc_ref, see_ref, dee_ref, soo_ref, doo_ref, xh_ref,
                 sz_ref, o_ref, *, gd, scale):
    rt, ct = tce_ref.shape[0], o_ref.shape[-1]
    k_base = pl.program_id(2) * rt
    for rows in _row_blocks(rt):
        k0 = k_base + rows.start + lax.broadcasted_iota(jnp.int32, (rows.stop - rows.start, ct), 0)
        nyquist = jnp.where((k0 & 1) == 0, scale, -scale) * xh_ref[0:1, :]
        er = _dot(tce_ref[rows, :], see_ref[...]) + nyquist
        ei = _dot(tse_ref[rows, :], dee_ref[...])
        orr = _dot(tco_ref[rows, :], soo_ref[...])
        oi = _dot(tso_ref[rows, :], doo_ref[...])
        for half, (gr, gi) in enumerate(((er + orr, ei + oi), (er - orr, ei - oi))):
            gr, gi = gr.astype(BF16), gi.astype(BF16)
            for g in range(ct // gd):
                cols = slice(g * gd, (g + 1) * gd)
                y = _dot(gr[:, cols], cc_ref[...]) + _dot(gi[:, cols], sc_ref[...])
                o_ref[half, rows, cols] = (y * sz_ref[half, rows, cols].astype(F32)).astype(o_ref.dtype)


def _fourier_mix(xe, xo, sz, tables, cc, sc, seq, rt, ct):
    t, d_b = sz.shape
    nb, half, quarter = t // seq, seq // 2, seq // 4
    tce, tse, tco, tso, pe, po = tables
    gd = cc.shape[0]
    nct = d_b // ct
    perm = pl.BlockSpec((quarter, quarter), lambda b, c: (0, 0))
    parity = pl.BlockSpec((half, ct), lambda b, c: (b, c))
    folded = pl.BlockSpec((quarter, ct), lambda b, c: (b, c))
    folded_shape = jax.ShapeDtypeStruct((t // 4, d_b), BF16)
    see, dee, soo, doo, xh = pl.pallas_call(
        _fold_kernel,
        grid=(nb, nct),
        in_specs=[perm, perm, parity, parity],
        out_specs=[folded] * 4 + [pl.BlockSpec((None, 8, ct), lambda b, c: (b, 0, c))],
        out_shape=[folded_shape] * 4 + [jax.ShapeDtypeStruct((nb, 8, d_b), F32)],
        compiler_params=_params("parallel", "parallel"),
        name="fourier_fold",
    )(pe, po, xe, xo)

    sz4 = sz.reshape(nb, 2, half, d_b)
    tab = pl.BlockSpec((rt, quarter), lambda b, c, r: (r, 0))
    chan = pl.BlockSpec((gd, gd), lambda b, c, r: (0, 0))
    rhs = pl.BlockSpec((quarter, ct), lambda b, c, r: (b, c))
    halves = pl.BlockSpec((None, 2, rt, ct), lambda b, c, r: (b, 0, r, c))
    out = pl.pallas_call(
        functools.partial(_fnet_kernel, gd=gd, scale=float(seq) ** -0.5),
        grid=(nb, nct, half // rt),
        in_specs=[tab, tab, tab, tab, chan, chan, rhs, rhs, rhs, rhs,
                  pl.BlockSpec((None, 8, ct), lambda b, c, r: (b, 0, c)), halves],
        out_specs=halves,
        out_shape=jax.ShapeDtypeStruct((nb, 2, half, d_b), BF16),
        compiler_params=_params("parallel", "parallel", "arbitrary"),
        name="fourier_dft",
    )(tce, tse, tco, tso, cc, sc, see, dee, soo, doo, xh, sz4)
    return out.reshape(t, d_b)


def _merge_kernel(a_ref, f_ref, wa_ref, wb_ref, sa_ref, sb_ref, o_ref):
    for rows in _row_blocks(a_ref.shape[0]):
        ya = _dot(a_ref[rows, :], wa_ref[...])
        yb = _dot(f_ref[rows, :], wb_ref[...])
        m = sa_ref[rows, :].astype(F32) * ya + sb_ref[rows, :].astype(F32) * yb
        o_ref[rows, :] = m.astype(o_ref.dtype)


def _merge(a, fb, w_a, w_b, layer, sg, bm, bn):
    t, d_a = a.shape
    d_b = fb.shape[1]
    d = w_a.shape[2]
    nd = d // bn
    return pl.pallas_call(
        _merge_kernel,
        grid=(t // bm, nd),
        in_specs=[pl.BlockSpec((bm, d_a), lambda i, j: (i, 0)),
                  pl.BlockSpec((bm, d_b), lambda i, j: (i, 0)),
                  pl.BlockSpec((None, d_a, bn), lambda i, j: (layer, 0, j)),
                  pl.BlockSpec((None, d_b, bn), lambda i, j: (layer, 0, j)),
                  pl.BlockSpec((bm, bn), lambda i, j: (i, j)),
                  pl.BlockSpec((bm, bn), lambda i, j: (i, nd + j))],
        out_specs=pl.BlockSpec((bm, bn), lambda i, j: (i, j)),
        out_shape=jax.ShapeDtypeStruct((t, d), BF16),
        compiler_params=_params("parallel", "arbitrary"),
        name="branch_merge",
    )(a, fb, w_a, w_b, sg, sg)


def _out_kernel(m_ref, w_ref, x_ref, xo_ref, xb_ref, r_ref, ssq_ref, *, n_tiles, d):
    j = pl.program_id(1)
    for rows in _row_blocks(m_ref.shape[0]):
        xn = x_ref[rows, :] + _dot(m_ref[rows, :], w_ref[...])
        xo_ref[rows, :] = xn
        xb_ref[rows, :] = xn.astype(xb_ref.dtype)
        sq = xn * xn
        ssq_ref[j, rows, :] = sum(sq[:, k:k + LANES] for k in range(0, sq.shape[1], LANES))

    @pl.when(j == n_tiles - 1)
    def _():
        for rows in _row_blocks(m_ref.shape[0]):
            tot = sum(ssq_ref[t, rows, :] for t in range(n_tiles))
            ms = jnp.sum(tot, axis=-1, keepdims=True) / d
            r_ref[rows, :] = jnp.broadcast_to(lax.rsqrt(ms + EPS), tot.shape)


def _out_proj(m, w_out, layer, x, bm, bn):
    t, d = m.shape
    n_tiles = d // bn
    tile = pl.BlockSpec((bm, bn), lambda i, j: (i, j))
    return pl.pallas_call(
        functools.partial(_out_kernel, n_tiles=n_tiles, d=d),
        grid=(t // bm, n_tiles),
        in_specs=[pl.BlockSpec((bm, d), lambda i, j: (i, 0)),
                  pl.BlockSpec((None, d, bn), lambda i, j: (layer, 0, j)),
                  tile],
        out_specs=[tile, tile, pl.BlockSpec((bm, LANES), lambda i, j: (i, 0))],
        out_shape=[jax.ShapeDtypeStruct((t, d), F32), jax.ShapeDtypeStruct((t, d), BF16),
                   jax.ShapeDtypeStruct((t, LANES), F32)],
        scratch_shapes=[pltpu.VMEM((n_tiles, bm, LANES), F32)],
        compiler_params=_params("parallel", "arbitrary"),
        name="out_proj",
    )(m, w_out, x)


def _out_final_kernel(m_ref, w_ref, x_ref, g_ref, y_ref, *, bn, d):
    m = m_ref[...]
    ssq = jnp.zeros((m.shape[0], LANES), F32)
    col_tiles = [slice(c, c + bn) for c in range(0, d, bn)]
    for cols in col_tiles:
        xn = x_ref[:, cols] + _dot(m, w_ref[:, cols])
        y_ref[:, cols] = xn
        sq = xn * xn
        ssq = ssq + sum(sq[:, k:k + LANES] for k in range(0, bn, LANES))
    scale = lax.rsqrt(jnp.sum(ssq, axis=-1, keepdims=True) / d + EPS)
    for cols in col_tiles:
        y_ref[:, cols] = y_ref[:, cols] * scale * g_ref[:, cols]


def _out_proj_final(m, w_out, layer, x, g, bm, bn):
    t, d = m.shape
    rows_full = pl.BlockSpec((bm, d), lambda i: (i, 0))
    vmem = d * d * 2 + 2 * bm * d * (2 + 4 + 4) + bm * d * 2 + 8 * bm * bn * 4
    return pl.pallas_call(
        functools.partial(_out_final_kernel, bn=bn, d=d),
        grid=(t // bm,),
        in_specs=[rows_full,
                  pl.BlockSpec((None, d, d), lambda i: (layer, 0, 0), pipeline_mode=pl.Buffered(1)),
                  rows_full,
                  pl.BlockSpec((1, d), lambda i: (0, 0))],
        out_specs=rows_full,
        out_shape=jax.ShapeDtypeStruct((t, d), F32),
        compiler_params=pltpu.CompilerParams(dimension_semantics=("parallel",), vmem_limit_bytes=vmem),
        name="out_proj_final_norm",
    )(m, w_out, x, g.reshape(1, d))


def _cos_sin(k, m, n, scale):
    ang = ((k * m) % n).astype(F32) * np.float32(2.0 * np.pi / n)
    return jnp.cos(ang) * np.float32(scale), jnp.sin(ang) * np.float32(-scale)


def _seq_tables(seq):
    half, quarter = seq // 2, seq // 4
    k = jnp.arange(half, dtype=jnp.int32)[:, None]
    m = jnp.arange(quarter, dtype=jnp.int32)[None, :]
    ce, se = _cos_sin(k, 2 * m, seq, seq ** -0.5)
    cphi, nsphi = _cos_sin(k, 1, seq, 1.0)
    co = ce * cphi - se * nsphi
    so = se * cphi + ce * nsphi
    i = jnp.arange(quarter, dtype=jnp.int32)
    ij = i[:, None] + i[None, :]
    mirror_even = ij == quarter
    mirror_odd = ij == quarter - 1
    return tuple(a.astype(BF16) for a in (ce, se, co, so, mirror_even, mirror_odd))


def _trunk(x3, weights, tables):
    (w_in, sgu_ln_g, sgu_ln_b, w_s, b_s_full, w_a, w_b, b_gate, w_out, final_g, cc, sc) = weights
    bsz, seq, d = x3.shape
    t = bsz * seq
    depth, d_a, d_b = w_a.shape[0], w_a.shape[1], w_b.shape[1]
    bm = _tile(t, ROW_TILE)
    bn = _tile(d_a, COL_TILE)
    bn2 = _tile(d_a, 2 * COL_TILE)
    rt = _tile(seq // 2, 512 if seq > 2048 else 1024)
    x = x3.reshape(t, d)
    h, rstd = _prep(x, _tile(t, 512))
    for l in range(depth):
        gv, mu, rs = _gelu_v(h, rstd, w_in, l, d_a, d_a, bm, bn2)
        a = _gate_a(h, rstd, w_in, l, 0, 2 * d_a, gv, mu, rs, sgu_ln_g[l], sgu_ln_b[l], w_s, b_s_full, bm, bn)
        xe, xo, sz = _fourier_in(h, rstd, w_in, l, 3 * d_a, 3 * d_a + d_b, d_b, bm, bn)
        sg = _gates(h, rstd, w_in, l, 3 * d_a + 2 * d_b, b_gate[l].reshape(1, 2 * d), bm, bn2)
        fb = _fourier_mix(xe, xo, sz, tables, cc, sc, seq, rt, bn)
        m = _merge(a, fb, w_a, w_b, l, sg, bm, bn2)
        if l + 1 < depth:
            x, h, rstd = _out_proj(m, w_out, l, x, bm, bn)
        else:
            y = _out_proj_final(m, w_out, l, x, final_g, _tile(t, ROW_SUB), bn)
    return y.reshape(bsz, seq, d)


def kernel(x_prompt, x_sample, norm_g, w_in, sgu_ln_g, sgu_ln_b, w_spatial, b_spatial, w_a, w_b, b_gate, w_out, final_g):
    d_b = w_b.shape[1]
    gd = d_b // B_GROUPS
    hd = w_a.shape[1] // b_spatial.shape[1]
    i = jnp.arange(gd, dtype=jnp.int32)
    cc, nsc = _cos_sin(i[:, None], i[None, :], gd, gd ** -0.5)
    b_s_full = jnp.broadcast_to(b_spatial[..., None], b_spatial.shape + (hd,))
    w_in_g = (norm_g[:, :, None] * w_in).astype(BF16)
    weights = (w_in_g, sgu_ln_g, sgu_ln_b, w_spatial.astype(BF16), b_s_full,
               w_a.astype(BF16), w_b.astype(BF16), b_gate, w_out.astype(BF16), final_g,
               cc.astype(BF16), (-nsc).astype(BF16))
    return tuple(_trunk(x3, weights, _seq_tables(x3.shape[1])) for x3 in (x_prompt, x_sample))
```

```python
import functools

import numpy as np
import jax
import jax.numpy as jnp
from jax import lax
from jax.experimental import pallas as pl
from jax.experimental.pallas import tpu as pltpu

B_GROUPS = 8
EPS = 1e-6
VMEM_LIMIT_BYTES = 56 * 1024 * 1024
ROW_TILE = 1024
COL_TILE = 512
ROW_SUB = 256
LANES = 128
F32 = jnp.float32
BF16 = jnp.bfloat16


def _params(*sem):
    return pltpu.CompilerParams(dimension_semantics=sem, vmem_limit_bytes=VMEM_LIMIT_BYTES)


def _dot(a, b):
    return jnp.dot(a, b, preferred_element_type=F32)


def _row_blocks(n_rows):
    rs = ROW_SUB if n_rows % ROW_SUB == 0 else n_rows
    return [slice(r, r + rs) for r in range(0, n_rows, rs)]


def _tile(n, pref):
    return pref if n % pref == 0 else n


def _rowwide(r, n):
    reps = n // r.shape[1]
    return jnp.concatenate([r] * reps, axis=1) if reps > 1 else r


def _rowscale(p, r):
    return p * _rowwide(r, p.shape[1])


def _lane_fold(x):
    return sum(x[:, k:k + LANES] for k in range(0, x.shape[1], LANES))


def _gelu_v_kernel(h_ref, r_ref, w_ref, gv_ref, mu_ref, rs_ref, s1_scr, s2_scr, *, n_tiles, d_a):
    _gelu_v_body(h_ref, r_ref, w_ref, gv_ref, mu_ref, rs_ref, s1_scr, s2_scr, n_tiles, d_a)


def _gelu_v_first_kernel(x_ref, w_ref, gv_ref, mu_ref, rs_ref, h_ref, r_ref, s1_scr, s2_scr, *, n_tiles, d_a, bn):
    d = x_ref.shape[1]

    @pl.when(pl.program_id(1) == 0)
    def _():
        for rows in _row_blocks(x_ref.shape[0]):
            ssq = jnp.zeros((rows.stop - rows.start, LANES), F32)
            for c in range(0, d, bn):
                x = x_ref[rows, c:c + bn]
                h_ref[rows, c:c + bn] = x.astype(h_ref.dtype)
                ssq = ssq + _lane_fold(x * x)
            ms = jnp.sum(ssq, axis=-1, keepdims=True) / d
            r_ref[rows, :] = jnp.broadcast_to(lax.rsqrt(ms + EPS), ssq.shape)

    _gelu_v_body(h_ref, r_ref, w_ref, gv_ref, mu_ref, rs_ref, s1_scr, s2_scr, n_tiles, d_a)


def _gelu_v_body(h_ref, r_ref, w_ref, gv_ref, mu_ref, rs_ref, s1_scr, s2_scr, n_tiles, d_a):
    j = pl.program_id(1)
    for rows in _row_blocks(h_ref.shape[0]):
        gv = jax.nn.gelu(_rowscale(_dot(h_ref[rows, :], w_ref[...]), r_ref[rows, :]))
        gv_ref[rows, :] = gv
        s1_scr[j, rows, :] = _lane_fold(gv)
        s2_scr[j, rows, :] = _lane_fold(gv * gv)

    @pl.when(j == n_tiles - 1)
    def _():
        for rows in _row_blocks(h_ref.shape[0]):
            s1 = jnp.sum(sum(s1_scr[t, rows, :] for t in range(n_tiles)), axis=-1, keepdims=True)
            s2 = jnp.sum(sum(s2_scr[t, rows, :] for t in range(n_tiles)), axis=-1, keepdims=True)
            mu = s1 / d_a
            var = jnp.maximum(s2 / d_a - mu * mu, 0.0)
            mu_ref[rows, :] = jnp.broadcast_to(mu, mu_ref[rows, :].shape)
            rs_ref[rows, :] = jnp.broadcast_to(lax.rsqrt(var + EPS), rs_ref[rows, :].shape)


def _gelu_v(h, rstd, w_in, layer, col0, d_a, bm, bn):
    t, d = h.shape
    n_tiles = d_a // bn
    blk0 = col0 // bn
    stat = pl.BlockSpec((bm, LANES), lambda i, j: (i, 0))
    stat_shape = jax.ShapeDtypeStruct((t, LANES), F32)
    return pl.pallas_call(
        functools.partial(_gelu_v_kernel, n_tiles=n_tiles, d_a=d_a),
        grid=(t // bm, n_tiles),
        in_specs=[pl.BlockSpec((bm, d), lambda i, j: (i, 0)),
                  stat,
                  pl.BlockSpec((None, d, bn), lambda i, j: (layer, 0, blk0 + j))],
        out_specs=[pl.BlockSpec((bm, bn), lambda i, j: (i, j)), stat, stat],
        out_shape=[jax.ShapeDtypeStruct((t, d_a), F32), stat_shape, stat_shape],
        scratch_shapes=[pltpu.VMEM((n_tiles, bm, LANES), F32), pltpu.VMEM((n_tiles, bm, LANES), F32)],
        compiler_params=_params("parallel", "arbitrary"),
        name="inproj_gelu_v",
    )(h, rstd, w_in)


def _gelu_v_first(x, w_in, layer, col0, d_a, bm, bn):
    t, d = x.shape
    n_tiles = d_a // bn
    blk0 = col0 // bn
    stat = pl.BlockSpec((bm, LANES), lambda i, j: (i, 0))
    stat_shape = jax.ShapeDtypeStruct((t, LANES), F32)
    rows_full = pl.BlockSpec((bm, d), lambda i, j: (i, 0))
    return pl.pallas_call(
        functools.partial(_gelu_v_first_kernel, n_tiles=n_tiles, d_a=d_a, bn=bn),
        grid=(t // bm, n_tiles),
        in_specs=[rows_full,
                  pl.BlockSpec((None, d, bn), lambda i, j: (layer, 0, blk0 + j))],
        out_specs=[pl.BlockSpec((bm, bn), lambda i, j: (i, j)), stat, stat, rows_full, stat],
        out_shape=[jax.ShapeDtypeStruct((t, d_a), F32), stat_shape, stat_shape,
                   jax.ShapeDtypeStruct((t, d), BF16), stat_shape],
        scratch_shapes=[pltpu.VMEM((n_tiles, bm, LANES), F32), pltpu.VMEM((n_tiles, bm, LANES), F32)],
        compiler_params=_params("parallel", "arbitrary"),
        name="inproj_gelu_v_first",
    )(x, w_in)


def _gate_a_kernel(h_ref, r_ref, wu_ref, wz_ref, gv_ref, mu_ref, rs_ref, lng_ref, lnb_ref, ws_ref, bs_ref, o_ref, *,
                   chunk, hd):
    for rows in _row_blocks(h_ref.shape[0]):
        h, r = h_ref[rows, :], r_ref[rows, :]
        u = jax.nn.gelu(_rowscale(_dot(h, wu_ref[...]), r))
        z = jax.nn.silu(_rowscale(_dot(h, wz_ref[...]), r))
        vn = _rowscale(gv_ref[rows, :] - _rowwide(mu_ref[rows, :], u.shape[1]), rs_ref[rows, :])
        vn = (vn * lng_ref[...] + lnb_ref[...]).astype(BF16)
        n_chunks = vn.shape[0] // chunk
        per_head = []
        for hh in range(vn.shape[1] // hd):
            cols = slice(hh * hd, (hh + 1) * hd)
            rhs = jnp.concatenate([vn[c * chunk:(c + 1) * chunk, cols] for c in range(n_chunks)], axis=1)
            res = _dot(ws_ref[hh], rhs)
            per_head.append(jnp.concatenate(
                [res[:, c * hd:(c + 1) * hd] + bs_ref[hh] for c in range(n_chunks)], axis=0))
        mixed = jnp.concatenate(per_head, axis=1)
        o_ref[rows, :] = (u * mixed * z).astype(o_ref.dtype)


def _gate_a(h, rstd, w_in, layer, col_u, col_z, gv, mu, rs, ln_g, ln_b, w_s, b_s_full, bm, bn):
    t, d = h.shape
    d_a = gv.shape[1]
    _, heads, chunk, _ = w_s.shape
    hd = d_a // heads
    hpt = bn // hd
    assert all((b.stop - b.start) % chunk == 0 for b in _row_blocks(bm)), "row sub-blocks must hold whole chunks"
    bu, bz = col_u // bn, col_z // bn
    stat = pl.BlockSpec((bm, LANES), lambda i, j: (i, 0))
    tile = pl.BlockSpec((bm, bn), lambda i, j: (i, j))
    vec = pl.BlockSpec((1, bn), lambda i, j: (0, j))
    return pl.pallas_call(
        functools.partial(_gate_a_kernel, chunk=chunk, hd=hd),
        grid=(t // bm, d_a // bn),
        in_specs=[pl.BlockSpec((bm, d), lambda i, j: (i, 0)),
                  stat,
                  pl.BlockSpec((None, d, bn), lambda i, j: (layer, 0, bu + j)),
                  pl.BlockSpec((None, d, bn), lambda i, j: (layer, 0, bz + j)),
                  tile, stat, stat, vec, vec,
                  pl.BlockSpec((None, hpt, chunk, chunk), lambda i, j: (layer, j, 0, 0)),
                  pl.BlockSpec((None, hpt, chunk, hd), lambda i, j: (layer, j, 0, 0))],
        out_specs=tile,
        out_shape=jax.ShapeDtypeStruct((t, d_a), BF16),
        compiler_params=_params("parallel", "arbitrary"),
        name="inproj_gate_a",
    )(h, rstd, w_in, w_in, gv, mu, rs, ln_g.reshape(1, d_a), ln_b.reshape(1, d_a), w_s, b_s_full)


def _fourier_in_kernel(h_ref, r_ref, wx_ref, wz_ref, xe_ref, xo_ref, sz_ref, x_scr):
    for rows in _row_blocks(h_ref.shape[0]):
        h, r = h_ref[rows, :], r_ref[rows, :]
        xb = _rowscale(_dot(h, wx_ref[...]), r)
        n_half = xb.shape[0] // 2
        half_rows = slice(rows.start // 2, rows.start // 2 + n_half)
        for q in range(xb.shape[1] // LANES):
            cols = slice(q * LANES, (q + 1) * LANES)
            x_scr[q, rows, :] = xb[:, cols]
            xe_ref[half_rows, cols] = x_scr[q, pl.ds(rows.start, n_half, stride=2), :].astype(xe_ref.dtype)
            xo_ref[half_rows, cols] = x_scr[q, pl.ds(rows.start + 1, n_half, stride=2), :].astype(xo_ref.dtype)
        sz_ref[rows, :] = jax.nn.silu(_rowscale(_dot(h, wz_ref[...]), r)).astype(sz_ref.dtype)


def _fourier_in(h, rstd, w_in, layer, col_x, col_z, d_b, bm, bn):
    t, d = h.shape
    bx, bz = col_x // bn, col_z // bn
    half = jax.ShapeDtypeStruct((t // 2, d_b), BF16)
    half_tile = pl.BlockSpec((bm // 2, bn), lambda i, j: (i, j))
    return pl.pallas_call(
        _fourier_in_kernel,
        grid=(t // bm, d_b // bn),
        in_specs=[pl.BlockSpec((bm, d), lambda i, j: (i, 0)),
                  pl.BlockSpec((bm, LANES), lambda i, j: (i, 0)),
                  pl.BlockSpec((None, d, bn), lambda i, j: (layer, 0, bx + j)),
                  pl.BlockSpec((None, d, bn), lambda i, j: (layer, 0, bz + j))],
        out_specs=[half_tile, half_tile, pl.BlockSpec((bm, bn), lambda i, j: (i, j))],
        out_shape=[half, half, jax.ShapeDtypeStruct((t, d_b), BF16)],
        scratch_shapes=[pltpu.VMEM((bn // LANES, bm, LANES), F32)],
        compiler_params=_params("parallel", "arbitrary"),
        name="inproj_fourier",
    )(h, rstd, w_in, w_in)


def _sig_kernel(h_ref, r_ref, w_ref, b_ref, o_ref):
    for rows in _row_blocks(h_ref.shape[0]):
        g = _rowscale(_dot(h_ref[rows, :], w_ref[...]), r_ref[rows, :]) + b_ref[...]
        o_ref[rows, :] = jax.nn.sigmoid(g).astype(o_ref.dtype)


def _gates(h, rstd, w_in, layer, col0, b_gate_flat, bm, bn):
    t, d = h.shape
    n = b_gate_flat.shape[1]
    blk0 = col0 // bn
    return pl.pallas_call(
        _sig_kernel,
        grid=(t // bm, n // bn),
        in_specs=[pl.BlockSpec((bm, d), lambda i, j: (i, 0)),
                  pl.BlockSpec((bm, LANES), lambda i, j: (i, 0)),
                  pl.BlockSpec((None, d, bn), lambda i, j: (layer, 0, blk0 + j)),
                  pl.BlockSpec((1, bn), lambda i, j: (0, j))],
        out_specs=pl.BlockSpec((bm, bn), lambda i, j: (i, j)),
        out_shape=jax.ShapeDtypeStruct((t, n), BF16),
        compiler_params=_params("parallel", "arbitrary"),
        name="inproj_gates",
    )(h, rstd, w_in, b_gate_flat)


def _fold_kernel(pe_ref, po_ref, xe_ref, xo_ref, see_ref, dee_ref, soo_ref, doo_ref, xh_ref):
    q = pe_ref.shape[0]
    for x_ref, p_ref, s_ref, d_ref in ((xe_ref, pe_ref, see_ref, dee_ref), (xo_ref, po_ref, soo_ref, doo_ref)):
        for rows in _row_blocks(q):
            lo = x_ref[rows, :].astype(F32)
            mirror = _dot(p_ref[rows, :], x_ref[q:, :])
            s_ref[rows, :] = (lo + mirror).astype(s_ref.dtype)
            d_ref[rows, :] = (lo - mirror).astype(d_ref.dtype)
    xh_ref[...] = jnp.broadcast_to(xe_ref[q:q + 1, :].astype(F32), xh_ref.shape)


def _fnet_kernel(tce_ref, tse_ref, tco_ref, tso_ref, cc_ref, sc_ref, see_ref, dee_ref, soo_ref, doo_ref, xh_ref,
                 sz_ref, o_ref, *, gd, scale):
    rt, ct = tce_ref.shape[0], o_ref.shape[-1]
    k_base = pl.program_id(2) * rt
    for rows in _row_blocks(rt):
        k0 = k_base + rows.start + lax.broadcasted_iota(jnp.int32, (rows.stop - rows.start, ct), 0)
        nyquist = jnp.where((k0 & 1) == 0, scale, -scale) * xh_ref[0:1, :]
        er = _dot(tce_ref[rows, :], see_ref[...]) + nyquist
        ei = _dot(tse_ref[rows, :], dee_ref[...])
        orr = _dot(tco_ref[rows, :], soo_ref[...])
        oi = _dot(tso_ref[rows, :], doo_ref[...])
        for half, (gr, gi) in enumerate(((er + orr, ei + oi), (er - orr, ei - oi))):
            gr, gi = gr.astype(BF16), gi.astype(BF16)
            for g in range(ct // gd):
                cols = slice(g * gd, (g + 1) * gd)
                y = _dot(gr[:, cols], cc_ref[...]) + _dot(gi[:, cols], sc_ref[...])
                o_ref[half, rows, cols] = (y * sz_ref[half, rows, cols].astype(F32)).astype(o_ref.dtype)


def _fourier_mix(xe, xo, sz, tables, cc, sc, seq, rt, ct):
    t, d_b = sz.shape
    nb, half, quarter = t // seq, seq // 2, seq // 4
    tce, tse, tco, tso, pe, po = tables
    gd = cc.shape[0]
    nct = d_b // ct
    perm = pl.BlockSpec((quarter, quarter), lambda b, c: (0, 0))
    parity = pl.BlockSpec((half, ct), lambda b, c: (b, c))
    folded = pl.BlockSpec((quarter, ct), lambda b, c: (b, c))
    folded_shape = jax.ShapeDtypeStruct((t // 4, d_b), BF16)
    see, dee, soo, doo, xh = pl.pallas_call(
        _fold_kernel,
        grid=(nb, nct),
        in_specs=[perm, perm, parity, parity],
        out_specs=[folded] * 4 + [pl.BlockSpec((None, 8, ct), lambda b, c: (b, 0, c))],
        out_shape=[folded_shape] * 4 + [jax.ShapeDtypeStruct((nb, 8, d_b), F32)],
        compiler_params=_params("parallel", "parallel"),
        name="fourier_fold",
    )(pe, po, xe, xo)

    sz4 = sz.reshape(nb, 2, half, d_b)
    tab = pl.BlockSpec((rt, quarter), lambda b, c, r: (r, 0))
    chan = pl.BlockSpec((gd, gd), lambda b, c, r: (0, 0))
    rhs = pl.BlockSpec((quarter, ct), lambda b, c, r: (b, c))
    halves = pl.BlockSpec((None, 2, rt, ct), lambda b, c, r: (b, 0, r, c))
    out = pl.pallas_call(
        functools.partial(_fnet_kernel, gd=gd, scale=float(seq) ** -0.5),
        grid=(nb, nct, half // rt),
        in_specs=[tab, tab, tab, tab, chan, chan, rhs, rhs, rhs, rhs,
                  pl.BlockSpec((None, 8, ct), lambda b, c, r: (b, 0, c)), halves],
        out_specs=halves,
        out_shape=jax.ShapeDtypeStruct((nb, 2, half, d_b), BF16),
        compiler_params=_params("parallel", "parallel", "arbitrary"),
        name="fourier_dft",
    )(tce, tse, tco, tso, cc, sc, see, dee, soo, doo, xh, sz4)
    return out.reshape(t, d_b)


def _merge_kernel(a_ref, f_ref, wa_ref, wb_ref, sa_ref, sb_ref, o_ref):
    for rows in _row_blocks(a_ref.shape[0]):
        ya = _dot(a_ref[rows, :], wa_ref[...])
        yb = _dot(f_ref[rows, :], wb_ref[...])
        m = sa_ref[rows, :].astype(F32) * ya + sb_ref[rows, :].astype(F32) * yb
        o_ref[rows, :] = m.astype(o_ref.dtype)


def _merge(a, fb, w_a, w_b, layer, sg, bm, bn):
    t, d_a = a.shape
    d_b = fb.shape[1]
    d = w_a.shape[2]
    nd = d // bn
    return pl.pallas_call(
        _merge_kernel,
        grid=(t // bm, nd),
        in_specs=[pl.BlockSpec((bm, d_a), lambda i, j: (i, 0)),
                  pl.BlockSpec((bm, d_b), lambda i, j: (i, 0)),
                  pl.BlockSpec((None, d_a, bn), lambda i, j: (layer, 0, j)),
                  pl.BlockSpec((None, d_b, bn), lambda i, j: (layer, 0, j)),
                  pl.BlockSpec((bm, bn), lambda i, j: (i, j)),
                  pl.BlockSpec((bm, bn), lambda i, j: (i, nd + j))],
        out_specs=pl.BlockSpec((bm, bn), lambda i, j: (i, j)),
        out_shape=jax.ShapeDtypeStruct((t, d), BF16),
        compiler_params=_params("parallel", "arbitrary"),
        name="branch_merge",
    )(a, fb, w_a, w_b, sg, sg)


def _out_kernel(m_ref, w_ref, x_ref, xo_ref, xb_ref, r_ref, ssq_ref, *, n_tiles, d):
    j = pl.program_id(1)
    for rows in _row_blocks(m_ref.shape[0]):
        xn = x_ref[rows, :] + _dot(m_ref[rows, :], w_ref[...])
        xo_ref[rows, :] = xn
        xb_ref[rows, :] = xn.astype(xb_ref.dtype)
        sq = xn * xn
        ssq_ref[j, rows, :] = sum(sq[:, k:k + LANES] for k in range(0, sq.shape[1], LANES))

    @pl.when(j == n_tiles - 1)
    def _():
        for rows in _row_blocks(m_ref.shape[0]):
            tot = sum(ssq_ref[t, rows, :] for t in range(n_tiles))
            ms = jnp.sum(tot, axis=-1, keepdims=True) / d
            r_ref[rows, :] = jnp.broadcast_to(lax.rsqrt(ms + EPS), tot.shape)


def _out_proj(m, w_out, layer, x, bm, bn):
    t, d = m.shape
    n_tiles = d // bn
    tile = pl.BlockSpec((bm, bn), lambda i, j: (i, j))
    return pl.pallas_call(
        functools.partial(_out_kernel, n_tiles=n_tiles, d=d),
        grid=(t // bm, n_tiles),
        in_specs=[pl.BlockSpec((bm, d), lambda i, j: (i, 0)),
                  pl.BlockSpec((None, d, bn), lambda i, j: (layer, 0, j)),
                  tile],
        out_specs=[tile, tile, pl.BlockSpec((bm, LANES), lambda i, j: (i, 0))],
        out_shape=[jax.ShapeDtypeStruct((t, d), F32), jax.ShapeDtypeStruct((t, d), BF16),
                   jax.ShapeDtypeStruct((t, LANES), F32)],
        scratch_shapes=[pltpu.VMEM((n_tiles, bm, LANES), F32)],
        compiler_params=_params("parallel", "arbitrary"),
        name="out_proj",
    )(m, w_out, x)


def _out_final_kernel(m_ref, w_ref, x_ref, g_ref, y_ref, *, bn, d):
    m = m_ref[...]
    ssq = jnp.zeros((m.shape[0], LANES), F32)
    col_tiles = [slice(c, c + bn) for c in range(0, d, bn)]
    for cols in col_tiles:
        xn = x_ref[:, cols] + _dot(m, w_ref[:, cols])
        y_ref[:, cols] = xn
        sq = xn * xn
        ssq = ssq + sum(sq[:, k:k + LANES] for k in range(0, bn, LANES))
    scale = lax.rsqrt(jnp.sum(ssq, axis=-1, keepdims=True) / d + EPS)
    for cols in col_tiles:
        y_ref[:, cols] = y_ref[:, cols] * scale * g_ref[:, cols]


def _out_proj_final(m, w_out, layer, x, g, bm, bn):
    t, d = m.shape
    rows_full = pl.BlockSpec((bm, d), lambda i: (i, 0))
    vmem = d * d * 2 + 2 * bm * d * (2 + 4 + 4) + bm * d * 2 + 8 * bm * bn * 4
    return pl.pallas_call(
        functools.partial(_out_final_kernel, bn=bn, d=d),
        grid=(t // bm,),
        in_specs=[rows_full,
                  pl.BlockSpec((None, d, d), lambda i: (layer, 0, 0), pipeline_mode=pl.Buffered(1)),
                  rows_full,
                  pl.BlockSpec((1, d), lambda i: (0, 0))],
        out_specs=rows_full,
        out_shape=jax.ShapeDtypeStruct((t, d), F32),
        compiler_params=pltpu.CompilerParams(dimension_semantics=("parallel",), vmem_limit_bytes=vmem),
        name="out_proj_final_norm",
    )(m, w_out, x, g.reshape(1, d))


def _cos_sin(k, m, n, scale):
    ang = ((k * m) % n).astype(F32) * np.float32(2.0 * np.pi / n)
    return jnp.cos(ang) * np.float32(scale), jnp.sin(ang) * np.float32(-scale)


def _seq_tables(seq):
    half, quarter = seq // 2, seq // 4
    k = jnp.arange(half, dtype=jnp.int32)[:, None]
    m = jnp.arange(quarter, dtype=jnp.int32)[None, :]
    ce, se = _cos_sin(k, 2 * m, seq, seq ** -0.5)
    cphi, nsphi = _cos_sin(k, 1, seq, 1.0)
    co = ce * cphi - se * nsphi
    so = se * cphi + ce * nsphi
    i = jnp.arange(quarter, dtype=jnp.int32)
    ij = i[:, None] + i[None, :]
    mirror_even = ij == quarter
    mirror_odd = ij == quarter - 1
    return tuple(a.astype(BF16) for a in (ce, se, co, so, mirror_even, mirror_odd))


def _trunk(x3, weights, tables):
    (w_in, sgu_ln_g, sgu_ln_b, w_s, b_s_full, w_a, w_b, b_gate, w_out, final_g, cc, sc) = weights
    bsz, seq, d = x3.shape
    t = bsz * seq
    depth, d_a, d_b = w_a.shape[0], w_a.shape[1], w_b.shape[1]
    bm = _tile(t, ROW_TILE)
    bn = _tile(d_a, COL_TILE)
    bn2 = _tile(d_a, 2 * COL_TILE)
    rt = _tile(seq // 2, 512 if seq > 2048 else 1024)
    x = x3.reshape(t, d)
    for l in range(depth):
        if l == 0:
            gv, mu, rs, h, rstd = _gelu_v_first(x, w_in, l, d_a, d_a, _tile(t, ROW_TILE // 2), bn2)
        else:
            gv, mu, rs = _gelu_v(h, rstd, w_in, l, d_a, d_a, bm, bn2)
        a = _gate_a(h, rstd, w_in, l, 0, 2 * d_a, gv, mu, rs, sgu_ln_g[l], sgu_ln_b[l], w_s, b_s_full, bm, bn)
        xe, xo, sz = _fourier_in(h, rstd, w_in, l, 3 * d_a, 3 * d_a + d_b, d_b, bm, bn)
        sg = _gates(h, rstd, w_in, l, 3 * d_a + 2 * d_b, b_gate[l].reshape(1, 2 * d), bm, bn2)
        fb = _fourier_mix(xe, xo, sz, tables, cc, sc, seq, rt, bn)
        m = _merge(a, fb, w_a, w_b, l, sg, bm, bn2)
        if l + 1 < depth:
            x, h, rstd = _out_proj(m, w_out, l, x, bm, bn)
        else:
            y = _out_proj_final(m, w_out, l, x, final_g, _tile(t, ROW_SUB), bn)
    return y.reshape(bsz, seq, d)


def kernel(x_prompt, x_sample, norm_g, w_in, sgu_ln_g, sgu_ln_b, w_spatial, b_spatial, w_a, w_b, b_gate, w_out, final_g):
    d_b = w_b.shape[1]
    gd = d_b // B_GROUPS
    hd = w_a.shape[1] // b_spatial.shape[1]
    i = jnp.arange(gd, dtype=jnp.int32)
    cc, nsc = _cos_sin(i[:, None], i[None, :], gd, gd ** -0.5)
    b_s_full = jnp.broadcast_to(b_spatial[..., None], b_spatial.shape + (hd,))
    w_in_g = (norm_g[:, :, None] * w_in).astype(BF16)
    weights = (w_in_g, sgu_ln_g, sgu_ln_b, w_spatial.astype(BF16), b_s_full,
               w_a.astype(BF16), w_b.astype(BF16), b_gate, w_out.astype(BF16), final_g,
               cc.astype(BF16), (-nsc).astype(BF16))
    return tuple(_trunk(x3, weights, _seq_tables(x3.shape[1])) for x3 in (x_prompt, x_sample))
```

```python
import functools

import numpy as np
import jax
import jax.numpy as jnp
from jax import lax
from jax.experimental import pallas as pl
from jax.experimental.pallas import tpu as pltpu

B_GROUPS = 8
EPS = 1e-6
VMEM_LIMIT_BYTES = 56 * 1024 * 1024
ROW_TILE = 1024
COL_TILE = 512
ROW_SUB = 256
LANES = 128
F32 = jnp.float32
BF16 = jnp.bfloat16


def _params(*sem):
    return pltpu.CompilerParams(dimension_semantics=sem, vmem_limit_bytes=VMEM_LIMIT_BYTES)


def _dot(a, b):
    return jnp.dot(a, b, preferred_element_type=F32)


def _row_blocks(n_rows):
    rs = ROW_SUB if n_rows % ROW_SUB == 0 else n_rows
    return [slice(r, r + rs) for r in range(0, n_rows, rs)]


def _tile(n, pref):
    return pref if n % pref == 0 else n


def _rowwide(r, n):
    reps = n // r.shape[1]
    return jnp.concatenate([r] * reps, axis=1) if reps > 1 else r


def _rowscale(p, r):
    return p * _rowwide(r, p.shape[1])


def _lane_fold(x):
    return sum(x[:, k:k + LANES] for k in range(0, x.shape[1], LANES))


def _gelu_v_kernel(h_ref, r_ref, w_ref, gv_ref, mu_ref, rs_ref, *, bn):
    _gelu_v_body(h_ref, r_ref, w_ref, gv_ref, mu_ref, rs_ref, bn)


def _gelu_v_first_kernel(x_ref, w_ref, gv_ref, mu_ref, rs_ref, h_ref, r_ref, *, bn):
    d = x_ref.shape[1]
    for rows in _row_blocks(x_ref.shape[0]):
        ssq = jnp.zeros((rows.stop - rows.start, LANES), F32)
        for c in range(0, d, bn):
            x = x_ref[rows, c:c + bn]
            h_ref[rows, c:c + bn] = x.astype(h_ref.dtype)
            ssq = ssq + _lane_fold(x * x)
        ms = jnp.sum(ssq, axis=-1, keepdims=True) / d
        r_ref[rows, :] = jnp.broadcast_to(lax.rsqrt(ms + EPS), ssq.shape)
    _gelu_v_body(h_ref, r_ref, w_ref, gv_ref, mu_ref, rs_ref, bn)


def _gelu_v_body(h_ref, r_ref, w_ref, gv_ref, mu_ref, rs_ref, bn):
    d_a = w_ref.shape[1]
    for rows in _row_blocks(h_ref.shape[0]):
        h, r = h_ref[rows, :], r_ref[rows, :]
        s1 = s2 = jnp.zeros((rows.stop - rows.start, LANES), F32)
        for c in range(0, d_a, bn):
            gv = jax.nn.gelu(_rowscale(_dot(h, w_ref[:, c:c + bn]), r))
            gv_ref[rows, c:c + bn] = gv
            s1 = s1 + _lane_fold(gv)
            s2 = s2 + _lane_fold(gv * gv)
        mu = jnp.sum(s1, axis=-1, keepdims=True) / d_a
        ex2 = jnp.sum(s2, axis=-1, keepdims=True) / d_a
        var = jnp.maximum(ex2 - mu * mu, 0.0)
        mu_ref[rows, :] = jnp.broadcast_to(mu, s1.shape)
        rs_ref[rows, :] = jnp.broadcast_to(lax.rsqrt(var + EPS), s1.shape)


def _gelu_v_specs(t, d, d_a, layer, col0, bm):
    stat = pl.BlockSpec((bm, LANES), lambda i: (i, 0))
    weight = pl.BlockSpec((None, d, d_a), lambda i: (layer, 0, col0 // d_a), pipeline_mode=pl.Buffered(1))
    rows_in = pl.BlockSpec((bm, d), lambda i: (i, 0))
    gv = pl.BlockSpec((bm, d_a), lambda i: (i, 0))
    stat_shape = jax.ShapeDtypeStruct((t, LANES), F32)
    return stat, weight, rows_in, gv, stat_shape


def _gelu_v(h, rstd, w_in, layer, col0, d_a, bm, bn):
    t, d = h.shape
    stat, weight, rows_in, gv, stat_shape = _gelu_v_specs(t, d, d_a, layer, col0, bm)
    return pl.pallas_call(
        functools.partial(_gelu_v_kernel, bn=bn),
        grid=(t // bm,),
        in_specs=[rows_in, stat, weight],
        out_specs=[gv, stat, stat],
        out_shape=[jax.ShapeDtypeStruct((t, d_a), F32), stat_shape, stat_shape],
        compiler_params=_params("parallel"),
        name="inproj_gelu_v",
    )(h, rstd, w_in)


def _gelu_v_first(x, w_in, layer, col0, d_a, bm, bn):
    t, d = x.shape
    stat, weight, rows_in, gv, stat_shape = _gelu_v_specs(t, d, d_a, layer, col0, bm)
    return pl.pallas_call(
        functools.partial(_gelu_v_first_kernel, bn=bn),
        grid=(t // bm,),
        in_specs=[rows_in, weight],
        out_specs=[gv, stat, stat, rows_in, stat],
        out_shape=[jax.ShapeDtypeStruct((t, d_a), F32), stat_shape, stat_shape,
                   jax.ShapeDtypeStruct((t, d), BF16), stat_shape],
        compiler_params=_params("parallel"),
        name="inproj_gelu_v_first",
    )(x, w_in)


def _gate_a_kernel(h_ref, r_ref, wu_ref, wz_ref, gv_ref, mu_ref, rs_ref, lng_ref, lnb_ref, ws_ref, bs_ref, o_ref, *,
                   chunk, hd):
    for rows in _row_blocks(h_ref.shape[0]):
        h, r = h_ref[rows, :], r_ref[rows, :]
        u = jax.nn.gelu(_rowscale(_dot(h, wu_ref[...]), r))
        z = jax.nn.silu(_rowscale(_dot(h, wz_ref[...]), r))
        vn = _rowscale(gv_ref[rows, :] - _rowwide(mu_ref[rows, :], u.shape[1]), rs_ref[rows, :])
        vn = (vn * lng_ref[...] + lnb_ref[...]).astype(BF16)
        n_chunks = vn.shape[0] // chunk
        per_head = []
        for hh in range(vn.shape[1] // hd):
            cols = slice(hh * hd, (hh + 1) * hd)
            rhs = jnp.concatenate([vn[c * chunk:(c + 1) * chunk, cols] for c in range(n_chunks)], axis=1)
            res = _dot(ws_ref[hh], rhs)
            per_head.append(jnp.concatenate(
                [res[:, c * hd:(c + 1) * hd] + bs_ref[hh] for c in range(n_chunks)], axis=0))
        mixed = jnp.concatenate(per_head, axis=1)
        o_ref[rows, :] = (u * mixed * z).astype(o_ref.dtype)


def _gate_a(h, rstd, w_in, layer, col_u, col_z, gv, mu, rs, ln_g, ln_b, w_s, b_s_full, bm, bn):
    t, d = h.shape
    d_a = gv.shape[1]
    _, heads, chunk, _ = w_s.shape
    hd = d_a // heads
    hpt = bn // hd
    assert all((b.stop - b.start) % chunk == 0 for b in _row_blocks(bm)), "row sub-blocks must hold whole chunks"
    bu, bz = col_u // bn, col_z // bn
    stat = pl.BlockSpec((bm, LANES), lambda i, j: (i, 0))
    tile = pl.BlockSpec((bm, bn), lambda i, j: (i, j))
    vec = pl.BlockSpec((1, bn), lambda i, j: (0, j))
    return pl.pallas_call(
        functools.partial(_gate_a_kernel, chunk=chunk, hd=hd),
        grid=(t // bm, d_a // bn),
        in_specs=[pl.BlockSpec((bm, d), lambda i, j: (i, 0)),
                  stat,
                  pl.BlockSpec((None, d, bn), lambda i, j: (layer, 0, bu + j)),
                  pl.BlockSpec((None, d, bn), lambda i, j: (layer, 0, bz + j)),
                  tile, stat, stat, vec, vec,
                  pl.BlockSpec((None, hpt, chunk, chunk), lambda i, j: (layer, j, 0, 0)),
                  pl.BlockSpec((None, hpt, chunk, hd), lambda i, j: (layer, j, 0, 0))],
        out_specs=tile,
        out_shape=jax.ShapeDtypeStruct((t, d_a), BF16),
        compiler_params=_params("parallel", "arbitrary"),
        name="inproj_gate_a",
    )(h, rstd, w_in, w_in, gv, mu, rs, ln_g.reshape(1, d_a), ln_b.reshape(1, d_a), w_s, b_s_full)


def _fourier_in_kernel(h_ref, r_ref, wx_ref, wz_ref, xe_ref, xo_ref, sz_ref, x_scr):
    for rows in _row_blocks(h_ref.shape[0]):
        h, r = h_ref[rows, :], r_ref[rows, :]
        xb = _rowscale(_dot(h, wx_ref[...]), r)
        n_half = xb.shape[0] // 2
        half_rows = slice(rows.start // 2, rows.start // 2 + n_half)
        for q in range(xb.shape[1] // LANES):
            cols = slice(q * LANES, (q + 1) * LANES)
            x_scr[q, rows, :] = xb[:, cols]
            xe_ref[half_rows, cols] = x_scr[q, pl.ds(rows.start, n_half, stride=2), :].astype(xe_ref.dtype)
            xo_ref[half_rows, cols] = x_scr[q, pl.ds(rows.start + 1, n_half, stride=2), :].astype(xo_ref.dtype)
        sz_ref[rows, :] = jax.nn.silu(_rowscale(_dot(h, wz_ref[...]), r)).astype(sz_ref.dtype)


def _fourier_in(h, rstd, w_in, layer, col_x, col_z, d_b, bm, bn):
    t, d = h.shape
    bx, bz = col_x // bn, col_z // bn
    half = jax.ShapeDtypeStruct((t // 2, d_b), BF16)
    half_tile = pl.BlockSpec((bm // 2, bn), lambda i, j: (i, j))
    return pl.pallas_call(
        _fourier_in_kernel,
        grid=(t // bm, d_b // bn),
        in_specs=[pl.BlockSpec((bm, d), lambda i, j: (i, 0)),
                  pl.BlockSpec((bm, LANES), lambda i, j: (i, 0)),
                  pl.BlockSpec((None, d, bn), lambda i, j: (layer, 0, bx + j)),
                  pl.BlockSpec((None, d, bn), lambda i, j: (layer, 0, bz + j))],
        out_specs=[half_tile, half_tile, pl.BlockSpec((bm, bn), lambda i, j: (i, j))],
        out_shape=[half, half, jax.ShapeDtypeStruct((t, d_b), BF16)],
        scratch_shapes=[pltpu.VMEM((bn // LANES, bm, LANES), F32)],
        compiler_params=_params("parallel", "arbitrary"),
        name="inproj_fourier",
    )(h, rstd, w_in, w_in)


def _sig_kernel(h_ref, r_ref, w_ref, b_ref, o_ref):
    for rows in _row_blocks(h_ref.shape[0]):
        g = _rowscale(_dot(h_ref[rows, :], w_ref[...]), r_ref[rows, :]) + b_ref[...]
        o_ref[rows, :] = jax.nn.sigmoid(g).astype(o_ref.dtype)


def _gates(h, rstd, w_in, layer, col0, b_gate_flat, bm, bn):
    t, d = h.shape
    n = b_gate_flat.shape[1]
    blk0 = col0 // bn
    return pl.pallas_call(
        _sig_kernel,
        grid=(t // bm, n // bn),
        in_specs=[pl.BlockSpec((bm, d), lambda i, j: (i, 0)),
                  pl.BlockSpec((bm, LANES), lambda i, j: (i, 0)),
                  pl.BlockSpec((None, d, bn), lambda i, j: (layer, 0, blk0 + j)),
                  pl.BlockSpec((1, bn), lambda i, j: (0, j))],
        out_specs=pl.BlockSpec((bm, bn), lambda i, j: (i, j)),
        out_shape=jax.ShapeDtypeStruct((t, n), BF16),
        compiler_params=_params("parallel", "arbitrary"),
        name="inproj_gates",
    )(h, rstd, w_in, b_gate_flat)


def _fold_kernel(pe_ref, po_ref, xe_ref, xo_ref, see_ref, dee_ref, soo_ref, doo_ref, xh_ref):
    q = pe_ref.shape[0]
    for x_ref, p_ref, s_ref, d_ref in ((xe_ref, pe_ref, see_ref, dee_ref), (xo_ref, po_ref, soo_ref, doo_ref)):
        for rows in _row_blocks(q):
            lo = x_ref[rows, :].astype(F32)
            mirror = _dot(p_ref[rows, :], x_ref[q:, :])
            s_ref[rows, :] = (lo + mirror).astype(s_ref.dtype)
            d_ref[rows, :] = (lo - mirror).astype(d_ref.dtype)
    xh_ref[...] = jnp.broadcast_to(xe_ref[q:q + 1, :].astype(F32), xh_ref.shape)


def _fnet_kernel(tce_ref, tse_ref, tco_ref, tso_ref, cc_ref, sc_ref, see_ref, dee_ref, soo_ref, doo_ref, xh_ref,
                 sz_ref, o_ref, *, gd, scale):
    rt, ct = tce_ref.shape[0], o_ref.shape[-1]
    k_base = pl.program_id(2) * rt
    for rows in _row_blocks(rt):
        k0 = k_base + rows.start + lax.broadcasted_iota(jnp.int32, (rows.stop - rows.start, ct), 0)
        nyquist = jnp.where((k0 & 1) == 0, scale, -scale) * xh_ref[0:1, :]
        er = _dot(tce_ref[rows, :], see_ref[...]) + nyquist
        ei = _dot(tse_ref[rows, :], dee_ref[...])
        orr = _dot(tco_ref[rows, :], soo_ref[...])
        oi = _dot(tso_ref[rows, :], doo_ref[...])
        for half, (gr, gi) in enumerate(((er + orr, ei + oi), (er - orr, ei - oi))):
            gr, gi = gr.astype(BF16), gi.astype(BF16)
            for g in range(ct // gd):
                cols = slice(g * gd, (g + 1) * gd)
                y = _dot(gr[:, cols], cc_ref[...]) + _dot(gi[:, cols], sc_ref[...])
                o_ref[half, rows, cols] = (y * sz_ref[half, rows, cols].astype(F32)).astype(o_ref.dtype)


def _fourier_mix(xe, xo, sz, tables, cc, sc, seq, rt, ct):
    t, d_b = sz.shape
    nb, half, quarter = t // seq, seq // 2, seq // 4
    tce, tse, tco, tso, pe, po = tables
    gd = cc.shape[0]
    nct = d_b // ct
    perm = pl.BlockSpec((quarter, quarter), lambda b, c: (0, 0))
    parity = pl.BlockSpec((half, ct), lambda b, c: (b, c))
    folded = pl.BlockSpec((quarter, ct), lambda b, c: (b, c))
    folded_shape = jax.ShapeDtypeStruct((t // 4, d_b), BF16)
    see, dee, soo, doo, xh = pl.pallas_call(
        _fold_kernel,
        grid=(nb, nct),
        in_specs=[perm, perm, parity, parity],
        out_specs=[folded] * 4 + [pl.BlockSpec((None, 8, ct), lambda b, c: (b, 0, c))],
        out_shape=[folded_shape] * 4 + [jax.ShapeDtypeStruct((nb, 8, d_b), F32)],
        compiler_params=_params("parallel", "parallel"),
        name="fourier_fold",
    )(pe, po, xe, xo)

    sz4 = sz.reshape(nb, 2, half, d_b)
    tab = pl.BlockSpec((rt, quarter), lambda b, c, r: (r, 0))
    chan = pl.BlockSpec((gd, gd), lambda b, c, r: (0, 0))
    rhs = pl.BlockSpec((quarter, ct), lambda b, c, r: (b, c))
    halves = pl.BlockSpec((None, 2, rt, ct), lambda b, c, r: (b, 0, r, c))
    out = pl.pallas_call(
        functools.partial(_fnet_kernel, gd=gd, scale=float(seq) ** -0.5),
        grid=(nb, nct, half // rt),
        in_specs=[tab, tab, tab, tab, chan, chan, rhs, rhs, rhs, rhs,
                  pl.BlockSpec((None, 8, ct), lambda b, c, r: (b, 0, c)), halves],
        out_specs=halves,
        out_shape=jax.ShapeDtypeStruct((nb, 2, half, d_b), BF16),
        compiler_params=_params("parallel", "parallel", "arbitrary"),
        name="fourier_dft",
    )(tce, tse, tco, tso, cc, sc, see, dee, soo, doo, xh, sz4)
    return out.reshape(t, d_b)


def _merge_kernel(a_ref, f_ref, wa_ref, wb_ref, sa_ref, sb_ref, o_ref):
    for rows in _row_blocks(a_ref.shape[0]):
        ya = _dot(a_ref[rows, :], wa_ref[...])
        yb = _dot(f_ref[rows, :], wb_ref[...])
        m = sa_ref[rows, :].astype(F32) * ya + sb_ref[rows, :].astype(F32) * yb
        o_ref[rows, :] = m.astype(o_ref.dtype)


def _merge(a, fb, w_a, w_b, layer, sg, bm, bn):
    t, d_a = a.shape
    d_b = fb.shape[1]
    d = w_a.shape[2]
    nd = d // bn
    return pl.pallas_call(
        _merge_kernel,
        grid=(t // bm, nd),
        in_specs=[pl.BlockSpec((bm, d_a), lambda i, j: (i, 0)),
                  pl.BlockSpec((bm, d_b), lambda i, j: (i, 0)),
                  pl.BlockSpec((None, d_a, bn), lambda i, j: (layer, 0, j)),
                  pl.BlockSpec((None, d_b, bn), lambda i, j: (layer, 0, j)),
                  pl.BlockSpec((bm, bn), lambda i, j: (i, j)),
                  pl.BlockSpec((bm, bn), lambda i, j: (i, nd + j))],
        out_specs=pl.BlockSpec((bm, bn), lambda i, j: (i, j)),
        out_shape=jax.ShapeDtypeStruct((t, d), BF16),
        compiler_params=_params("parallel", "arbitrary"),
        name="branch_merge",
    )(a, fb, w_a, w_b, sg, sg)


def _out_kernel(m_ref, w_ref, x_ref, xo_ref, xb_ref, r_ref, ssq_ref, *, n_tiles, d):
    j = pl.program_id(1)
    for rows in _row_blocks(m_ref.shape[0]):
        xn = x_ref[rows, :] + _dot(m_ref[rows, :], w_ref[...])
        xo_ref[rows, :] = xn
        xb_ref[rows, :] = xn.astype(xb_ref.dtype)
        sq = xn * xn
        ssq_ref[j, rows, :] = sum(sq[:, k:k + LANES] for k in range(0, sq.shape[1], LANES))

    @pl.when(j == n_tiles - 1)
    def _():
        for rows in _row_blocks(m_ref.shape[0]):
            tot = sum(ssq_ref[t, rows, :] for t in range(n_tiles))
            ms = jnp.sum(tot, axis=-1, keepdims=True) / d
            r_ref[rows, :] = jnp.broadcast_to(lax.rsqrt(ms + EPS), tot.shape)


def _out_proj(m, w_out, layer, x, bm, bn):
    t, d = m.shape
    n_tiles = d // bn
    tile = pl.BlockSpec((bm, bn), lambda i, j: (i, j))
    return pl.pallas_call(
        functools.partial(_out_kernel, n_tiles=n_tiles, d=d),
        grid=(t // bm, n_tiles),
        in_specs=[pl.BlockSpec((bm, d), lambda i, j: (i, 0)),
                  pl.BlockSpec((None, d, bn), lambda i, j: (layer, 0, j)),
                  tile],
        out_specs=[tile, tile, pl.BlockSpec((bm, LANES), lambda i, j: (i, 0))],
        out_shape=[jax.ShapeDtypeStruct((t, d), F32), jax.ShapeDtypeStruct((t, d), BF16),
                   jax.ShapeDtypeStruct((t, LANES), F32)],
        scratch_shapes=[pltpu.VMEM((n_tiles, bm, LANES), F32)],
        compiler_params=_params("parallel", "arbitrary"),
        name="out_proj",
    )(m, w_out, x)


def _out_final_kernel(m_ref, w_ref, x_ref, g_ref, y_ref, *, bn, d):
    m = m_ref[...]
    ssq = jnp.zeros((m.shape[0], LANES), F32)
    col_tiles = [slice(c, c + bn) for c in range(0, d, bn)]
    for cols in col_tiles:
        xn = x_ref[:, cols] + _dot(m, w_ref[:, cols])
        y_ref[:, cols] = xn
        sq = xn * xn
        ssq = ssq + sum(sq[:, k:k + LANES] for k in range(0, bn, LANES))
    scale = lax.rsqrt(jnp.sum(ssq, axis=-1, keepdims=True) / d + EPS)
    for cols in col_tiles:
        y_ref[:, cols] = y_ref[:, cols] * scale * g_ref[:, cols]


def _out_proj_final(m, w_out, layer, x, g, bm, bn):
    t, d = m.shape
    rows_full = pl.BlockSpec((bm, d), lambda i: (i, 0))
    vmem = d * d * 2 + 2 * bm * d * (2 + 4 + 4) + bm * d * 2 + 8 * bm * bn * 4
    return pl.pallas_call(
        functools.partial(_out_final_kernel, bn=bn, d=d),
        grid=(t // bm,),
        in_specs=[rows_full,
                  pl.BlockSpec((None, d, d), lambda i: (layer, 0, 0), pipeline_mode=pl.Buffered(1)),
                  rows_full,
                  pl.BlockSpec((1, d), lambda i: (0, 0))],
        out_specs=rows_full,
        out_shape=jax.ShapeDtypeStruct((t, d), F32),
        compiler_params=pltpu.CompilerParams(dimension_semantics=("parallel",), vmem_limit_bytes=vmem),
        name="out_proj_final_norm",
    )(m, w_out, x, g.reshape(1, d))


def _cos_sin(k, m, n, scale):
    ang = ((k * m) % n).astype(F32) * np.float32(2.0 * np.pi / n)
    return jnp.cos(ang) * np.float32(scale), jnp.sin(ang) * np.float32(-scale)


def _seq_tables(seq):
    half, quarter = seq // 2, seq // 4
    k = jnp.arange(half, dtype=jnp.int32)[:, None]
    m = jnp.arange(quarter, dtype=jnp.int32)[None, :]
    ce, se = _cos_sin(k, 2 * m, seq, seq ** -0.5)
    cphi, nsphi = _cos_sin(k, 1, seq, 1.0)
    co = ce * cphi - se * nsphi
    so = se * cphi + ce * nsphi
    i = jnp.arange(quarter, dtype=jnp.int32)
    ij = i[:, None] + i[None, :]
    mirror_even = ij == quarter
    mirror_odd = ij == quarter - 1
    return tuple(a.astype(BF16) for a in (ce, se, co, so, mirror_even, mirror_odd))


def _trunk(x3, weights, tables):
    (w_in, sgu_ln_g, sgu_ln_b, w_s, b_s_full, w_a, w_b, b_gate, w_out, final_g, cc, sc) = weights
    bsz, seq, d = x3.shape
    t = bsz * seq
    depth, d_a, d_b = w_a.shape[0], w_a.shape[1], w_b.shape[1]
    bm = _tile(t, ROW_TILE)
    bn = _tile(d_a, COL_TILE)
    bn2 = _tile(d_a, 2 * COL_TILE)
    rt = _tile(seq // 2, 512 if seq > 2048 else 1024)
    x = x3.reshape(t, d)
    for l in range(depth):
        if l == 0:
            gv, mu, rs, h, rstd = _gelu_v_first(x, w_in, l, d_a, d_a, _tile(t, ROW_TILE // 2), bn2)
        else:
            gv, mu, rs = _gelu_v(h, rstd, w_in, l, d_a, d_a, bm, bn2)
        a = _gate_a(h, rstd, w_in, l, 0, 2 * d_a, gv, mu, rs, sgu_ln_g[l], sgu_ln_b[l], w_s, b_s_full, bm, bn)
        xe, xo, sz = _fourier_in(h, rstd, w_in, l, 3 * d_a, 3 * d_a + d_b, d_b, bm, bn)
        sg = _gates(h, rstd, w_in, l, 3 * d_a + 2 * d_b, b_gate[l].reshape(1, 2 * d), bm, bn2)
        fb = _fourier_mix(xe, xo, sz, tables, cc, sc, seq, rt, bn)
        m = _merge(a, fb, w_a, w_b, l, sg, bm, bn2)
        if l + 1 < depth:
            x, h, rstd = _out_proj(m, w_out, l, x, bm, bn)
        else:
            y = _out_proj_final(m, w_out, l, x, final_g, _tile(t, ROW_SUB), bn)
    return y.reshape(bsz, seq, d)


def kernel(x_prompt, x_sample, norm_g, w_in, sgu_ln_g, sgu_ln_b, w_spatial, b_spatial, w_a, w_b, b_gate, w_out, final_g):
    d_b = w_b.shape[1]
    gd = d_b // B_GROUPS
    hd = w_a.shape[1] // b_spatial.shape[1]
    i = jnp.arange(gd, dtype=jnp.int32)
    cc, nsc = _cos_sin(i[:, None], i[None, :], gd, gd ** -0.5)
    b_s_full = jnp.broadcast_to(b_spatial[..., None], b_spatial.shape + (hd,))
    w_in_g = (norm_g[:, :, None] * w_in).astype(BF16)
    weights = (w_in_g, sgu_ln_g, sgu_ln_b, w_spatial.astype(BF16), b_s_full,
               w_a.astype(BF16), w_b.astype(BF16), b_gate, w_out.astype(BF16), final_g,
               cc.astype(BF16), (-nsc).astype(BF16))
    return tuple(_trunk(x3, weights, _seq_tables(x3.shape[1])) for x3 in (x_prompt, x_sample))
```

```python
import functools

import numpy as np
import jax
import jax.numpy as jnp
from jax import lax
from jax.experimental import pallas as pl
from jax.experimental.pallas import tpu as pltpu

B_GROUPS = 8
EPS = 1e-6
VMEM_LIMIT_BYTES = 56 * 1024 * 1024
ROW_TILE = 1024
COL_TILE = 512
ROW_SUB = 256
LANES = 128
F32 = jnp.float32
BF16 = jnp.bfloat16


def _params(*sem):
    return pltpu.CompilerParams(dimension_semantics=sem, vmem_limit_bytes=VMEM_LIMIT_BYTES)


def _dot(a, b):
    return jnp.dot(a, b, preferred_element_type=F32)


def _row_blocks(n_rows):
    rs = ROW_SUB if n_rows % ROW_SUB == 0 else n_rows
    return [slice(r, r + rs) for r in range(0, n_rows, rs)]


def _tile(n, pref):
    return pref if n % pref == 0 else n


def _rowwide(r, n):
    reps = n // r.shape[1]
    return jnp.concatenate([r] * reps, axis=1) if reps > 1 else r


def _rowscale(p, r):
    return p * _rowwide(r, p.shape[1])


def _lane_fold(x):
    return sum(x[:, k:k + LANES] for k in range(0, x.shape[1], LANES))


def _gelu_v_kernel(h_ref, r_ref, w_ref, gv_ref, mu_ref, rs_ref, *, bn):
    _gelu_v_body(h_ref, r_ref, w_ref, gv_ref, mu_ref, rs_ref, bn)


def _gelu_v_first_kernel(x_ref, w_ref, gv_ref, mu_ref, rs_ref, h_ref, r_ref, *, bn):
    d = x_ref.shape[1]
    for rows in _row_blocks(x_ref.shape[0]):
        ssq = jnp.zeros((rows.stop - rows.start, LANES), F32)
        for c in range(0, d, bn):
            x = x_ref[rows, c:c + bn]
            h_ref[rows, c:c + bn] = x.astype(h_ref.dtype)
            ssq = ssq + _lane_fold(x * x)
        ms = jnp.sum(ssq, axis=-1, keepdims=True) / d
        r_ref[rows, :] = jnp.broadcast_to(lax.rsqrt(ms + EPS), ssq.shape)
    _gelu_v_body(h_ref, r_ref, w_ref, gv_ref, mu_ref, rs_ref, bn)


def _gelu_v_body(h_ref, r_ref, w_ref, gv_ref, mu_ref, rs_ref, bn):
    d_a = w_ref.shape[1]
    for rows in _row_blocks(h_ref.shape[0]):
        h, r = h_ref[rows, :], r_ref[rows, :]
        s1 = s2 = jnp.zeros((rows.stop - rows.start, LANES), F32)
        for c in range(0, d_a, bn):
            gv = jax.nn.gelu(_rowscale(_dot(h, w_ref[:, c:c + bn]), r))
            gv_ref[rows, c:c + bn] = gv
            s1 = s1 + _lane_fold(gv)
            s2 = s2 + _lane_fold(gv * gv)
        mu = jnp.sum(s1, axis=-1, keepdims=True) / d_a
        ex2 = jnp.sum(s2, axis=-1, keepdims=True) / d_a
        var = jnp.maximum(ex2 - mu * mu, 0.0)
        mu_ref[rows, :] = jnp.broadcast_to(mu, s1.shape)
        rs_ref[rows, :] = jnp.broadcast_to(lax.rsqrt(var + EPS), s1.shape)


def _gelu_v_specs(t, d, d_a, layer, col0, bm):
    stat = pl.BlockSpec((bm, LANES), lambda i: (i, 0))
    weight = pl.BlockSpec((None, d, d_a), lambda i: (layer, 0, col0 // d_a), pipeline_mode=pl.Buffered(1))
    rows_in = pl.BlockSpec((bm, d), lambda i: (i, 0))
    gv = pl.BlockSpec((bm, d_a), lambda i: (i, 0))
    stat_shape = jax.ShapeDtypeStruct((t, LANES), F32)
    return stat, weight, rows_in, gv, stat_shape


def _gelu_v(h, rstd, w_in, layer, col0, d_a, bm, bn):
    t, d = h.shape
    stat, weight, rows_in, gv, stat_shape = _gelu_v_specs(t, d, d_a, layer, col0, bm)
    return pl.pallas_call(
        functools.partial(_gelu_v_kernel, bn=bn),
        grid=(t // bm,),
        in_specs=[rows_in, stat, weight],
        out_specs=[gv, stat, stat],
        out_shape=[jax.ShapeDtypeStruct((t, d_a), F32), stat_shape, stat_shape],
        compiler_params=_params("parallel"),
        name="inproj_gelu_v",
    )(h, rstd, w_in)


def _gelu_v_first(x, w_in, layer, col0, d_a, bm, bn):
    t, d = x.shape
    stat, weight, rows_in, gv, stat_shape = _gelu_v_specs(t, d, d_a, layer, col0, bm)
    return pl.pallas_call(
        functools.partial(_gelu_v_first_kernel, bn=bn),
        grid=(t // bm,),
        in_specs=[rows_in, weight],
        out_specs=[gv, stat, stat, rows_in, stat],
        out_shape=[jax.ShapeDtypeStruct((t, d_a), F32), stat_shape, stat_shape,
                   jax.ShapeDtypeStruct((t, d), BF16), stat_shape],
        compiler_params=_params("parallel"),
        name="inproj_gelu_v_first",
    )(x, w_in)


def _gate_a_kernel(h_ref, r_ref, wu_ref, wz_ref, gv_ref, mu_ref, rs_ref, lng_ref, lnb_ref, ws_ref, bs_ref, o_ref, *,
                   chunk, hd, bn):
    d_a = wu_ref.shape[1]
    for rows in _row_blocks(h_ref.shape[0]):
        h, r = h_ref[rows, :], r_ref[rows, :]
        mu, rs = _rowwide(mu_ref[rows, :], bn), _rowwide(rs_ref[rows, :], bn)
        n_chunks = (rows.stop - rows.start) // chunk
        for c0 in range(0, d_a, bn):
            tile = slice(c0, c0 + bn)
            u = jax.nn.gelu(_rowscale(_dot(h, wu_ref[:, tile]), r))
            z = jax.nn.silu(_rowscale(_dot(h, wz_ref[:, tile]), r))
            vn = (((gv_ref[rows, tile] - mu) * rs) * lng_ref[:, tile] + lnb_ref[:, tile]).astype(BF16)
            per_head = []
            for hh in range(bn // hd):
                head = c0 // hd + hh
                cols = slice(hh * hd, (hh + 1) * hd)
                rhs = jnp.concatenate([vn[c * chunk:(c + 1) * chunk, cols] for c in range(n_chunks)], axis=1)
                res = _dot(ws_ref[head], rhs)
                per_head.append(jnp.concatenate(
                    [res[:, c * hd:(c + 1) * hd] + bs_ref[head] for c in range(n_chunks)], axis=0))
            mixed = jnp.concatenate(per_head, axis=1)
            o_ref[rows, tile] = (u * mixed * z).astype(o_ref.dtype)


def _gate_a(h, rstd, w_in, layer, col_u, col_z, gv, mu, rs, ln_g, ln_b, w_s, b_s_full, bm, bn):
    t, d = h.shape
    d_a = gv.shape[1]
    _, heads, chunk, _ = w_s.shape
    hd = d_a // heads
    assert all((b.stop - b.start) % chunk == 0 for b in _row_blocks(bm)), "row sub-blocks must hold whole chunks"
    stat = pl.BlockSpec((bm, LANES), lambda i: (i, 0))
    vec = pl.BlockSpec((1, d_a), lambda i: (0, 0))
    return pl.pallas_call(
        functools.partial(_gate_a_kernel, chunk=chunk, hd=hd, bn=bn),
        grid=(t // bm,),
        in_specs=[pl.BlockSpec((bm, d), lambda i: (i, 0)),
                  stat,
                  _resident(d, d_a, layer, col_u),
                  _resident(d, d_a, layer, col_z),
                  pl.BlockSpec((bm, d_a), lambda i: (i, 0)), stat, stat, vec, vec,
                  pl.BlockSpec((None, heads, chunk, chunk), lambda i: (layer, 0, 0, 0)),
                  pl.BlockSpec((None, heads, chunk, hd), lambda i: (layer, 0, 0, 0))],
        out_specs=pl.BlockSpec((bm, d_a), lambda i: (i, 0)),
        out_shape=jax.ShapeDtypeStruct((t, d_a), BF16),
        compiler_params=_params("parallel"),
        name="inproj_gate_a",
    )(h, rstd, w_in, w_in, gv, mu, rs, ln_g.reshape(1, d_a), ln_b.reshape(1, d_a), w_s, b_s_full)


def _fourier_in_kernel(h_ref, r_ref, wx_ref, wz_ref, xe_ref, xo_ref, sz_ref, x_scr, *, bn):
    d_b = wx_ref.shape[1]
    for rows in _row_blocks(h_ref.shape[0]):
        h, r = h_ref[rows, :], r_ref[rows, :]
        n_half = (rows.stop - rows.start) // 2
        half_rows = slice(rows.start // 2, rows.start // 2 + n_half)
        for c in range(0, d_b, bn):
            xb = _rowscale(_dot(h, wx_ref[:, c:c + bn]), r)
            for q in range(bn // LANES):
                slot = c // LANES + q
                cols = slice(c + q * LANES, c + (q + 1) * LANES)
                x_scr[slot, rows, :] = xb[:, q * LANES:(q + 1) * LANES]
                xe_ref[half_rows, cols] = x_scr[slot, pl.ds(rows.start, n_half, stride=2), :].astype(xe_ref.dtype)
                xo_ref[half_rows, cols] = x_scr[slot, pl.ds(rows.start + 1, n_half, stride=2), :].astype(xo_ref.dtype)
            sz_ref[rows, c:c + bn] = jax.nn.silu(_rowscale(_dot(h, wz_ref[:, c:c + bn]), r)).astype(sz_ref.dtype)


def _resident(rows, cols, layer, col0):
    assert col0 % cols == 0
    return pl.BlockSpec((None, rows, cols), lambda i: (layer, 0, col0 // cols), pipeline_mode=pl.Buffered(1))


def _fourier_in(h, rstd, w_in, layer, col_x, col_z, d_b, bm, bn):
    t, d = h.shape
    half = jax.ShapeDtypeStruct((t // 2, d_b), BF16)
    half_rows = pl.BlockSpec((bm // 2, d_b), lambda i: (i, 0))
    return pl.pallas_call(
        functools.partial(_fourier_in_kernel, bn=bn),
        grid=(t // bm,),
        in_specs=[pl.BlockSpec((bm, d), lambda i: (i, 0)),
                  pl.BlockSpec((bm, LANES), lambda i: (i, 0)),
                  _resident(d, d_b, layer, col_x),
                  _resident(d, d_b, layer, col_z)],
        out_specs=[half_rows, half_rows, pl.BlockSpec((bm, d_b), lambda i: (i, 0))],
        out_shape=[half, half, jax.ShapeDtypeStruct((t, d_b), BF16)],
        scratch_shapes=[pltpu.VMEM((d_b // LANES, bm, LANES), F32)],
        compiler_params=_params("parallel"),
        name="inproj_fourier",
    )(h, rstd, w_in, w_in)


def _sig_kernel(h_ref, r_ref, w_ref, b_ref, o_ref):
    for rows in _row_blocks(h_ref.shape[0]):
        g = _rowscale(_dot(h_ref[rows, :], w_ref[...]), r_ref[rows, :]) + b_ref[...]
        o_ref[rows, :] = jax.nn.sigmoid(g).astype(o_ref.dtype)


def _gates(h, rstd, w_in, layer, col0, b_gate_flat, bm, bn):
    t, d = h.shape
    n = b_gate_flat.shape[1]
    blk0 = col0 // bn
    return pl.pallas_call(
        _sig_kernel,
        grid=(t // bm, n // bn),
        in_specs=[pl.BlockSpec((bm, d), lambda i, j: (i, 0)),
                  pl.BlockSpec((bm, LANES), lambda i, j: (i, 0)),
                  pl.BlockSpec((None, d, bn), lambda i, j: (layer, 0, blk0 + j)),
                  pl.BlockSpec((1, bn), lambda i, j: (0, j))],
        out_specs=pl.BlockSpec((bm, bn), lambda i, j: (i, j)),
        out_shape=jax.ShapeDtypeStruct((t, n), BF16),
        compiler_params=_params("parallel", "arbitrary"),
        name="inproj_gates",
    )(h, rstd, w_in, b_gate_flat)


def _fold_kernel(pe_ref, po_ref, xe_ref, xo_ref, see_ref, dee_ref, soo_ref, doo_ref, xh_ref):
    q = pe_ref.shape[0]
    for x_ref, p_ref, s_ref, d_ref in ((xe_ref, pe_ref, see_ref, dee_ref), (xo_ref, po_ref, soo_ref, doo_ref)):
        for rows in _row_blocks(q):
            lo = x_ref[rows, :].astype(F32)
            mirror = _dot(p_ref[rows, :], x_ref[q:, :])
            s_ref[rows, :] = (lo + mirror).astype(s_ref.dtype)
            d_ref[rows, :] = (lo - mirror).astype(d_ref.dtype)
    xh_ref[...] = jnp.broadcast_to(xe_ref[q:q + 1, :].astype(F32), xh_ref.shape)


def _fnet_kernel(tce_ref, tse_ref, tco_ref, tso_ref, cc_ref, sc_ref, see_ref, dee_ref, soo_ref, doo_ref, xh_ref,
                 sz_ref, o_ref, *, gd, scale):
    rt, ct = tce_ref.shape[0], o_ref.shape[-1]
    k_base = pl.program_id(2) * rt
    for rows in _row_blocks(rt):
        k0 = k_base + rows.start + lax.broadcasted_iota(jnp.int32, (rows.stop - rows.start, ct), 0)
        nyquist = jnp.where((k0 & 1) == 0, scale, -scale) * xh_ref[0:1, :]
        er = _dot(tce_ref[rows, :], see_ref[...]) + nyquist
        ei = _dot(tse_ref[rows, :], dee_ref[...])
        orr = _dot(tco_ref[rows, :], soo_ref[...])
        oi = _dot(tso_ref[rows, :], doo_ref[...])
        for half, (gr, gi) in enumerate(((er + orr, ei + oi), (er - orr, ei - oi))):
            gr, gi = gr.astype(BF16), gi.astype(BF16)
            for g in range(ct // gd):
                cols = slice(g * gd, (g + 1) * gd)
                y = _dot(gr[:, cols], cc_ref[...]) + _dot(gi[:, cols], sc_ref[...])
                o_ref[half, rows, cols] = (y * sz_ref[half, rows, cols].astype(F32)).astype(o_ref.dtype)


def _fourier_mix(xe, xo, sz, tables, cc, sc, seq, rt, ct):
    t, d_b = sz.shape
    nb, half, quarter = t // seq, seq // 2, seq // 4
    tce, tse, tco, tso, pe, po = tables
    gd = cc.shape[0]
    nct = d_b // ct
    perm = pl.BlockSpec((quarter, quarter), lambda b, c: (0, 0))
    parity = pl.BlockSpec((half, ct), lambda b, c: (b, c))
    folded = pl.BlockSpec((quarter, ct), lambda b, c: (b, c))
    folded_shape = jax.ShapeDtypeStruct((t // 4, d_b), BF16)
    see, dee, soo, doo, xh = pl.pallas_call(
        _fold_kernel,
        grid=(nb, nct),
        in_specs=[perm, perm, parity, parity],
        out_specs=[folded] * 4 + [pl.BlockSpec((None, 8, ct), lambda b, c: (b, 0, c))],
        out_shape=[folded_shape] * 4 + [jax.ShapeDtypeStruct((nb, 8, d_b), F32)],
        compiler_params=_params("parallel", "parallel"),
        name="fourier_fold",
    )(pe, po, xe, xo)

    sz4 = sz.reshape(nb, 2, half, d_b)
    tab = pl.BlockSpec((rt, quarter), lambda b, c, r: (r, 0))
    chan = pl.BlockSpec((gd, gd), lambda b, c, r: (0, 0))
    rhs = pl.BlockSpec((quarter, ct), lambda b, c, r: (b, c))
    halves = pl.BlockSpec((None, 2, rt, ct), lambda b, c, r: (b, 0, r, c))
    out = pl.pallas_call(
        functools.partial(_fnet_kernel, gd=gd, scale=float(seq) ** -0.5),
        grid=(nb, nct, half // rt),
        in_specs=[tab, tab, tab, tab, chan, chan, rhs, rhs, rhs, rhs,
                  pl.BlockSpec((None, 8, ct), lambda b, c, r: (b, 0, c)), halves],
        out_specs=halves,
        out_shape=jax.ShapeDtypeStruct((nb, 2, half, d_b), BF16),
        compiler_params=_params("parallel", "parallel", "arbitrary"),
        name="fourier_dft",
    )(tce, tse, tco, tso, cc, sc, see, dee, soo, doo, xh, sz4)
    return out.reshape(t, d_b)


def _merge_kernel(a_ref, f_ref, wa_ref, wb_ref, sa_ref, sb_ref, o_ref, *, bn):
    d = wa_ref.shape[1]
    for rows in _row_blocks(a_ref.shape[0]):
        a, f = a_ref[rows, :], f_ref[rows, :]
        for c in range(0, d, bn):
            cols = slice(c, c + bn)
            ya = _dot(a, wa_ref[:, cols])
            yb = _dot(f, wb_ref[:, cols])
            m = sa_ref[rows, cols].astype(F32) * ya + sb_ref[rows, cols].astype(F32) * yb
            o_ref[rows, cols] = m.astype(o_ref.dtype)


def _merge(a, fb, w_a, w_b, layer, sg, bm, bn):
    t, d_a = a.shape
    d_b = fb.shape[1]
    d = w_a.shape[2]
    return pl.pallas_call(
        functools.partial(_merge_kernel, bn=bn),
        grid=(t // bm,),
        in_specs=[pl.BlockSpec((bm, d_a), lambda i: (i, 0)),
                  pl.BlockSpec((bm, d_b), lambda i: (i, 0)),
                  _resident(d_a, d, layer, 0),
                  _resident(d_b, d, layer, 0),
                  pl.BlockSpec((bm, d), lambda i: (i, 0)),
                  pl.BlockSpec((bm, d), lambda i: (i, 1))],
        out_specs=pl.BlockSpec((bm, d), lambda i: (i, 0)),
        out_shape=jax.ShapeDtypeStruct((t, d), BF16),
        compiler_params=_params("parallel"),
        name="branch_merge",
    )(a, fb, w_a, w_b, sg, sg)


def _out_kernel(m_ref, w_ref, x_ref, xo_ref, xb_ref, r_ref, ssq_ref, *, n_tiles, d):
    j = pl.program_id(1)
    for rows in _row_blocks(m_ref.shape[0]):
        xn = x_ref[rows, :] + _dot(m_ref[rows, :], w_ref[...])
        xo_ref[rows, :] = xn
        xb_ref[rows, :] = xn.astype(xb_ref.dtype)
        sq = xn * xn
        ssq_ref[j, rows, :] = sum(sq[:, k:k + LANES] for k in range(0, sq.shape[1], LANES))

    @pl.when(j == n_tiles - 1)
    def _():
        for rows in _row_blocks(m_ref.shape[0]):
            tot = sum(ssq_ref[t, rows, :] for t in range(n_tiles))
            ms = jnp.sum(tot, axis=-1, keepdims=True) / d
            r_ref[rows, :] = jnp.broadcast_to(lax.rsqrt(ms + EPS), tot.shape)


def _out_proj(m, w_out, layer, x, bm, bn):
    t, d = m.shape
    n_tiles = d // bn
    tile = pl.BlockSpec((bm, bn), lambda i, j: (i, j))
    return pl.pallas_call(
        functools.partial(_out_kernel, n_tiles=n_tiles, d=d),
        grid=(t // bm, n_tiles),
        in_specs=[pl.BlockSpec((bm, d), lambda i, j: (i, 0)),
                  pl.BlockSpec((None, d, bn), lambda i, j: (layer, 0, j)),
                  tile],
        out_specs=[tile, tile, pl.BlockSpec((bm, LANES), lambda i, j: (i, 0))],
        out_shape=[jax.ShapeDtypeStruct((t, d), F32), jax.ShapeDtypeStruct((t, d), BF16),
                   jax.ShapeDtypeStruct((t, LANES), F32)],
        scratch_shapes=[pltpu.VMEM((n_tiles, bm, LANES), F32)],
        compiler_params=_params("parallel", "arbitrary"),
        name="out_proj",
    )(m, w_out, x)


def _out_final_kernel(m_ref, w_ref, x_ref, g_ref, y_ref, *, bn, d):
    m = m_ref[...]
    ssq = jnp.zeros((m.shape[0], LANES), F32)
    col_tiles = [slice(c, c + bn) for c in range(0, d, bn)]
    for cols in col_tiles:
        xn = x_ref[:, cols] + _dot(m, w_ref[:, cols])
        y_ref[:, cols] = xn
        sq = xn * xn
        ssq = ssq + sum(sq[:, k:k + LANES] for k in range(0, bn, LANES))
    scale = lax.rsqrt(jnp.sum(ssq, axis=-1, keepdims=True) / d + EPS)
    for cols in col_tiles:
        y_ref[:, cols] = y_ref[:, cols] * scale * g_ref[:, cols]


def _out_proj_final(m, w_out, layer, x, g, bm, bn):
    t, d = m.shape
    rows_full = pl.BlockSpec((bm, d), lambda i: (i, 0))
    vmem = d * d * 2 + 2 * bm * d * (2 + 4 + 4) + bm * d * 2 + 8 * bm * bn * 4
    return pl.pallas_call(
        functools.partial(_out_final_kernel, bn=bn, d=d),
        grid=(t // bm,),
        in_specs=[rows_full,
                  pl.BlockSpec((None, d, d), lambda i: (layer, 0, 0), pipeline_mode=pl.Buffered(1)),
                  rows_full,
                  pl.BlockSpec((1, d), lambda i: (0, 0))],
        out_specs=rows_full,
        out_shape=jax.ShapeDtypeStruct((t, d), F32),
        compiler_params=pltpu.CompilerParams(dimension_semantics=("parallel",), vmem_limit_bytes=vmem),
        name="out_proj_final_norm",
    )(m, w_out, x, g.reshape(1, d))


def _cos_sin(k, m, n, scale):
    ang = ((k * m) % n).astype(F32) * np.float32(2.0 * np.pi / n)
    return jnp.cos(ang) * np.float32(scale), jnp.sin(ang) * np.float32(-scale)


def _seq_tables(seq):
    half, quarter = seq // 2, seq // 4
    k = jnp.arange(half, dtype=jnp.int32)[:, None]
    m = jnp.arange(quarter, dtype=jnp.int32)[None, :]
    ce, se = _cos_sin(k, 2 * m, seq, seq ** -0.5)
    cphi, nsphi = _cos_sin(k, 1, seq, 1.0)
    co = ce * cphi - se * nsphi
    so = se * cphi + ce * nsphi
    i = jnp.arange(quarter, dtype=jnp.int32)
    ij = i[:, None] + i[None, :]
    mirror_even = ij == quarter
    mirror_odd = ij == quarter - 1
    return tuple(a.astype(BF16) for a in (ce, se, co, so, mirror_even, mirror_odd))


def _trunk(x3, weights, tables):
    (w_in, sgu_ln_g, sgu_ln_b, w_s, b_s_full, w_a, w_b, b_gate, w_out, final_g, cc, sc) = weights
    bsz, seq, d = x3.shape
    t = bsz * seq
    depth, d_a, d_b = w_a.shape[0], w_a.shape[1], w_b.shape[1]
    bm = _tile(t, ROW_TILE)
    bm_res = _tile(t, ROW_SUB)
    bn = _tile(d_a, COL_TILE)
    bn2 = _tile(d_a, 2 * COL_TILE)
    rt = _tile(seq // 2, 512 if seq > 2048 else 1024)
    x = x3.reshape(t, d)
    for l in range(depth):
        if l == 0:
            gv, mu, rs, h, rstd = _gelu_v_first(x, w_in, l, d_a, d_a, _tile(t, ROW_TILE // 2), bn2)
        else:
            gv, mu, rs = _gelu_v(h, rstd, w_in, l, d_a, d_a, bm, bn2)
        a = _gate_a(h, rstd, w_in, l, 0, 2 * d_a, gv, mu, rs, sgu_ln_g[l], sgu_ln_b[l], w_s, b_s_full, bm_res, bn)
        xe, xo, sz = _fourier_in(h, rstd, w_in, l, 3 * d_a, 3 * d_a + d_b, d_b, _tile(t, 2 * ROW_SUB), bn)
        sg = _gates(h, rstd, w_in, l, 3 * d_a + 2 * d_b, b_gate[l].reshape(1, 2 * d), bm, bn2)
        fb = _fourier_mix(xe, xo, sz, tables, cc, sc, seq, rt, bn)
        m = _merge(a, fb, w_a, w_b, l, sg, bm_res, bn2)
        if l + 1 < depth:
            x, h, rstd = _out_proj(m, w_out, l, x, bm, bn)
        else:
            y = _out_proj_final(m, w_out, l, x, final_g, bm_res, bn)
    return y.reshape(bsz, seq, d)


def kernel(x_prompt, x_sample, norm_g, w_in, sgu_ln_g, sgu_ln_b, w_spatial, b_spatial, w_a, w_b, b_gate, w_out, final_g):
    d_b = w_b.shape[1]
    gd = d_b // B_GROUPS
    hd = w_a.shape[1] // b_spatial.shape[1]
    i = jnp.arange(gd, dtype=jnp.int32)
    cc, nsc = _cos_sin(i[:, None], i[None, :], gd, gd ** -0.5)
    b_s_full = jnp.broadcast_to(b_spatial[..., None], b_spatial.shape + (hd,))
    w_in_g = (norm_g[:, :, None] * w_in).astype(BF16)
    weights = (w_in_g, sgu_ln_g, sgu_ln_b, w_spatial.astype(BF16), b_s_full,
               w_a.astype(BF16), w_b.astype(BF16), b_gate, w_out.astype(BF16), final_g,
               cc.astype(BF16), (-nsc).astype(BF16))
    return tuple(_trunk(x3, weights, _seq_tables(x3.shape[1])) for x3 in (x_prompt, x_sample))
```

```python
import functools

import numpy as np
import jax
import jax.numpy as jnp
from jax import lax
from jax.experimental import pallas as pl
from jax.experimental.pallas import tpu as pltpu

B_GROUPS = 8
EPS = 1e-6
VMEM_LIMIT_BYTES = 56 * 1024 * 1024
ROW_TILE = 1024
COL_TILE = 512
ROW_SUB = 256
LANES = 128
SUBLANES = 8
F32 = jnp.float32
BF16 = jnp.bfloat16


def _params(*sem):
    return pltpu.CompilerParams(dimension_semantics=sem, vmem_limit_bytes=VMEM_LIMIT_BYTES)


def _dot(a, b):
    return jnp.dot(a, b, preferred_element_type=F32)


def _row_blocks(n_rows):
    rs = ROW_SUB if n_rows % ROW_SUB == 0 else n_rows
    return [slice(r, r + rs) for r in range(0, n_rows, rs)]


def _tile(n, pref):
    return pref if n % pref == 0 else n


def _rowwide(r, n):
    reps = n // r.shape[1]
    return jnp.concatenate([r] * reps, axis=1) if reps > 1 else r


def _rowscale(p, r):
    return p * _rowwide(r, p.shape[1])


def _lane_fold(x):
    return sum(x[:, k:k + LANES] for k in range(0, x.shape[1], LANES))


def _gelu_v_kernel(h_ref, r_ref, w_ref, gv_ref, mu_ref, rs_ref, *, bn):
    _gelu_v_body(h_ref, r_ref, w_ref, gv_ref, mu_ref, rs_ref, bn)


def _gelu_v_first_kernel(x_ref, w_ref, gv_ref, mu_ref, rs_ref, h_ref, r_ref, *, bn):
    d = x_ref.shape[1]
    for rows in _row_blocks(x_ref.shape[0]):
        ssq = jnp.zeros((rows.stop - rows.start, LANES), F32)
        for c in range(0, d, bn):
            x = x_ref[rows, c:c + bn]
            h_ref[rows, c:c + bn] = x.astype(h_ref.dtype)
            ssq = ssq + _lane_fold(x * x)
        ms = jnp.sum(ssq, axis=-1, keepdims=True) / d
        r_ref[rows, :] = jnp.broadcast_to(lax.rsqrt(ms + EPS), ssq.shape)
    _gelu_v_body(h_ref, r_ref, w_ref, gv_ref, mu_ref, rs_ref, bn)


def _gelu_v_body(h_ref, r_ref, w_ref, gv_ref, mu_ref, rs_ref, bn):
    d_a = w_ref.shape[1]
    for rows in _row_blocks(h_ref.shape[0]):
        h, r = h_ref[rows, :], r_ref[rows, :]
        s1 = s2 = jnp.zeros((rows.stop - rows.start, LANES), F32)
        for c in range(0, d_a, bn):
            gv = jax.nn.gelu(_rowscale(_dot(h, w_ref[:, c:c + bn]), r))
            gv_ref[rows, c:c + bn] = gv
            s1 = s1 + _lane_fold(gv)
            s2 = s2 + _lane_fold(gv * gv)
        mu = jnp.sum(s1, axis=-1, keepdims=True) / d_a
        ex2 = jnp.sum(s2, axis=-1, keepdims=True) / d_a
        var = jnp.maximum(ex2 - mu * mu, 0.0)
        mu_ref[rows, :] = jnp.broadcast_to(mu, s1.shape)
        rs_ref[rows, :] = jnp.broadcast_to(lax.rsqrt(var + EPS), s1.shape)


def _gelu_v_specs(t, d, d_a, layer, col0, bm):
    stat = pl.BlockSpec((bm, LANES), lambda i: (i, 0))
    weight = pl.BlockSpec((None, d, d_a), lambda i: (layer, 0, col0 // d_a), pipeline_mode=pl.Buffered(1))
    rows_in = pl.BlockSpec((bm, d), lambda i: (i, 0))
    gv = pl.BlockSpec((bm, d_a), lambda i: (i, 0))
    stat_shape = jax.ShapeDtypeStruct((t, LANES), F32)
    return stat, weight, rows_in, gv, stat_shape


def _gelu_v(h, rstd, w_in, layer, col0, d_a, bm, bn):
    t, d = h.shape
    stat, weight, rows_in, gv, stat_shape = _gelu_v_specs(t, d, d_a, layer, col0, bm)
    return pl.pallas_call(
        functools.partial(_gelu_v_kernel, bn=bn),
        grid=(t // bm,),
        in_specs=[rows_in, stat, weight],
        out_specs=[gv, stat, stat],
        out_shape=[jax.ShapeDtypeStruct((t, d_a), F32), stat_shape, stat_shape],
        compiler_params=_params("parallel"),
        name="inproj_gelu_v",
    )(h, rstd, w_in)


def _gelu_v_first(x, w_in, layer, col0, d_a, bm, bn):
    t, d = x.shape
    stat, weight, rows_in, gv, stat_shape = _gelu_v_specs(t, d, d_a, layer, col0, bm)
    return pl.pallas_call(
        functools.partial(_gelu_v_first_kernel, bn=bn),
        grid=(t // bm,),
        in_specs=[rows_in, weight],
        out_specs=[gv, stat, stat, rows_in, stat],
        out_shape=[jax.ShapeDtypeStruct((t, d_a), F32), stat_shape, stat_shape,
                   jax.ShapeDtypeStruct((t, d), BF16), stat_shape],
        compiler_params=_params("parallel"),
        name="inproj_gelu_v_first",
    )(x, w_in)


def _gate_a_kernel(h_ref, r_ref, wu_ref, wz_ref, gv_ref, mu_ref, rs_ref, lng_ref, lnb_ref, ws_ref, bs_ref, o_ref, *,
                   chunk, hd):
    for rows in _row_blocks(h_ref.shape[0]):
        h, r = h_ref[rows, :], r_ref[rows, :]
        u = jax.nn.gelu(_rowscale(_dot(h, wu_ref[...]), r))
        z = jax.nn.silu(_rowscale(_dot(h, wz_ref[...]), r))
        vn = _rowscale(gv_ref[rows, :] - _rowwide(mu_ref[rows, :], u.shape[1]), rs_ref[rows, :])
        vn = (vn * lng_ref[...] + lnb_ref[...]).astype(BF16)
        n_chunks = vn.shape[0] // chunk
        per_head = []
        for hh in range(vn.shape[1] // hd):
            cols = slice(hh * hd, (hh + 1) * hd)
            rhs = jnp.concatenate([vn[c * chunk:(c + 1) * chunk, cols] for c in range(n_chunks)], axis=1)
            res = _dot(ws_ref[hh], rhs)
            per_head.append(jnp.concatenate(
                [res[:, c * hd:(c + 1) * hd] + bs_ref[hh] for c in range(n_chunks)], axis=0))
        mixed = jnp.concatenate(per_head, axis=1)
        o_ref[rows, :] = (u * mixed * z).astype(o_ref.dtype)


def _gate_a(h, rstd, w_in, layer, col_u, col_z, gv, mu, rs, ln_g, ln_b, w_s, b_s_full, bm, bn):
    t, d = h.shape
    d_a = gv.shape[1]
    _, heads, chunk, _ = w_s.shape
    hd = d_a // heads
    hpt = bn // hd
    assert all((b.stop - b.start) % chunk == 0 for b in _row_blocks(bm)), "row sub-blocks must hold whole chunks"
    bu, bz = col_u // bn, col_z // bn
    stat = pl.BlockSpec((bm, LANES), lambda i, j: (i, 0))
    tile = pl.BlockSpec((bm, bn), lambda i, j: (i, j))
    vec = pl.BlockSpec((1, bn), lambda i, j: (0, j))
    return pl.pallas_call(
        functools.partial(_gate_a_kernel, chunk=chunk, hd=hd),
        grid=(t // bm, d_a // bn),
        in_specs=[pl.BlockSpec((bm, d), lambda i, j: (i, 0)),
                  stat,
                  pl.BlockSpec((None, d, bn), lambda i, j: (layer, 0, bu + j)),
                  pl.BlockSpec((None, d, bn), lambda i, j: (layer, 0, bz + j)),
                  tile, stat, stat, vec, vec,
                  pl.BlockSpec((None, hpt, chunk, chunk), lambda i, j: (layer, j, 0, 0)),
                  pl.BlockSpec((None, hpt, chunk, hd), lambda i, j: (layer, j, 0, 0))],
        out_specs=tile,
        out_shape=jax.ShapeDtypeStruct((t, d_a), BF16),
        compiler_params=_params("parallel", "arbitrary"),
        name="inproj_gate_a",
    )(h, rstd, w_in, w_in, gv, mu, rs, ln_g.reshape(1, d_a), ln_b.reshape(1, d_a), w_s, b_s_full)


def _fourier_in_kernel(h_ref, r_ref, wx_ref, wz_ref, xe_ref, xo_ref, sz_ref, x_scr):
    for rows in _row_blocks(h_ref.shape[0]):
        h, r = h_ref[rows, :], r_ref[rows, :]
        xb = _rowscale(_dot(h, wx_ref[...]), r)
        n_half = xb.shape[0] // 2
        half_rows = slice(rows.start // 2, rows.start // 2 + n_half)
        for q in range(xb.shape[1] // LANES):
            cols = slice(q * LANES, (q + 1) * LANES)
            x_scr[q, rows, :] = xb[:, cols]
            xe_ref[half_rows, cols] = x_scr[q, pl.ds(rows.start, n_half, stride=2), :].astype(xe_ref.dtype)
            xo_ref[half_rows, cols] = x_scr[q, pl.ds(rows.start + 1, n_half, stride=2), :].astype(xo_ref.dtype)
        sz_ref[rows, :] = jax.nn.silu(_rowscale(_dot(h, wz_ref[...]), r)).astype(sz_ref.dtype)


def _fourier_in(h, rstd, w_in, layer, col_x, col_z, d_b, bm, bn):
    t, d = h.shape
    bx, bz = col_x // bn, col_z // bn
    half = jax.ShapeDtypeStruct((t // 2, d_b), BF16)
    half_tile = pl.BlockSpec((bm // 2, bn), lambda i, j: (i, j))
    return pl.pallas_call(
        _fourier_in_kernel,
        grid=(t // bm, d_b // bn),
        in_specs=[pl.BlockSpec((bm, d), lambda i, j: (i, 0)),
                  pl.BlockSpec((bm, LANES), lambda i, j: (i, 0)),
                  pl.BlockSpec((None, d, bn), lambda i, j: (layer, 0, bx + j)),
                  pl.BlockSpec((None, d, bn), lambda i, j: (layer, 0, bz + j))],
        out_specs=[half_tile, half_tile, pl.BlockSpec((bm, bn), lambda i, j: (i, j))],
        out_shape=[half, half, jax.ShapeDtypeStruct((t, d_b), BF16)],
        scratch_shapes=[pltpu.VMEM((bn // LANES, bm, LANES), F32)],
        compiler_params=_params("parallel", "arbitrary"),
        name="inproj_fourier",
    )(h, rstd, w_in, w_in)


def _sig_kernel(h_ref, r_ref, w_ref, b_ref, o_ref):
    for rows in _row_blocks(h_ref.shape[0]):
        g = _rowscale(_dot(h_ref[rows, :], w_ref[...]), r_ref[rows, :]) + b_ref[...]
        o_ref[rows, :] = jax.nn.sigmoid(g).astype(o_ref.dtype)


def _gates(h, rstd, w_in, layer, col0, b_gate_flat, bm, bn):
    t, d = h.shape
    n = b_gate_flat.shape[1]
    blk0 = col0 // bn
    return pl.pallas_call(
        _sig_kernel,
        grid=(t // bm, n // bn),
        in_specs=[pl.BlockSpec((bm, d), lambda i, j: (i, 0)),
                  pl.BlockSpec((bm, LANES), lambda i, j: (i, 0)),
                  pl.BlockSpec((None, d, bn), lambda i, j: (layer, 0, blk0 + j)),
                  pl.BlockSpec((1, bn), lambda i, j: (0, j))],
        out_specs=pl.BlockSpec((bm, bn), lambda i, j: (i, j)),
        out_shape=jax.ShapeDtypeStruct((t, n), BF16),
        compiler_params=_params("parallel", "arbitrary"),
        name="inproj_gates",
    )(h, rstd, w_in, b_gate_flat)


def _fold_kernel(pe_ref, po_ref, xe_ref, xo_ref, see_ref, dee_ref, soo_ref, doo_ref, xh_ref):
    q = pe_ref.shape[0]
    for x_ref, p_ref, s_ref, d_ref in ((xe_ref, pe_ref, see_ref, dee_ref), (xo_ref, po_ref, soo_ref, doo_ref)):
        for rows in _row_blocks(q):
            lo = x_ref[rows, :].astype(F32)
            mirror = _dot(p_ref[rows, :], x_ref[q:, :])
            s_ref[rows, :] = (lo + mirror).astype(s_ref.dtype)
            d_ref[rows, :] = (lo - mirror).astype(d_ref.dtype)
    xh_ref[...] = jnp.broadcast_to(xe_ref[q:q + 1, :].astype(F32), xh_ref.shape)


def _fnet_kernel(tce_ref, tse_ref, tco_ref, tso_ref, cc_ref, sc_ref, see_ref, dee_ref, soo_ref, doo_ref, xh_ref,
                 sz_ref, o_ref, *, gd, scale):
    rt, ct = tce_ref.shape[0], o_ref.shape[-1]
    k_base = pl.program_id(2) * rt
    for rows in _row_blocks(rt):
        k0 = k_base + rows.start + lax.broadcasted_iota(jnp.int32, (rows.stop - rows.start, ct), 0)
        nyquist = jnp.where((k0 & 1) == 0, scale, -scale) * xh_ref[0:1, :]
        er = _dot(tce_ref[rows, :], see_ref[...]) + nyquist
        ei = _dot(tse_ref[rows, :], dee_ref[...])
        orr = _dot(tco_ref[rows, :], soo_ref[...])
        oi = _dot(tso_ref[rows, :], doo_ref[...])
        for half, (gr, gi) in enumerate(((er + orr, ei + oi), (er - orr, ei - oi))):
            gr, gi = gr.astype(BF16), gi.astype(BF16)
            for g in range(ct // gd):
                cols = slice(g * gd, (g + 1) * gd)
                y = _dot(gr[:, cols], cc_ref[...]) + _dot(gi[:, cols], sc_ref[...])
                o_ref[half, rows, cols] = (y * sz_ref[half, rows, cols].astype(F32)).astype(o_ref.dtype)


def _fourier_mix(xe, xo, sz, tables, cc, sc, seq, rt, ct):
    t, d_b = sz.shape
    nb, half, quarter = t // seq, seq // 2, seq // 4
    tce, tse, tco, tso, pe, po = tables
    gd = cc.shape[0]
    nct = d_b // ct
    perm = pl.BlockSpec((quarter, quarter), lambda b, c: (0, 0))
    parity = pl.BlockSpec((half, ct), lambda b, c: (b, c))
    folded = pl.BlockSpec((quarter, ct), lambda b, c: (b, c))
    folded_shape = jax.ShapeDtypeStruct((t // 4, d_b), BF16)
    see, dee, soo, doo, xh = pl.pallas_call(
        _fold_kernel,
        grid=(nb, nct),
        in_specs=[perm, perm, parity, parity],
        out_specs=[folded] * 4 + [pl.BlockSpec((None, SUBLANES, ct), lambda b, c: (b, 0, c))],
        out_shape=[folded_shape] * 4 + [jax.ShapeDtypeStruct((nb, SUBLANES, d_b), F32)],
        compiler_params=_params("parallel", "parallel"),
        name="fourier_fold",
    )(pe, po, xe, xo)

    sz4 = sz.reshape(nb, 2, half, d_b)
    tab = pl.BlockSpec((rt, quarter), lambda b, c, r: (r, 0))
    chan = pl.BlockSpec((gd, gd), lambda b, c, r: (0, 0))
    rhs = pl.BlockSpec((quarter, ct), lambda b, c, r: (b, c))
    halves = pl.BlockSpec((None, 2, rt, ct), lambda b, c, r: (b, 0, r, c))
    out = pl.pallas_call(
        functools.partial(_fnet_kernel, gd=gd, scale=float(seq) ** -0.5),
        grid=(nb, nct, half // rt),
        in_specs=[tab, tab, tab, tab, chan, chan, rhs, rhs, rhs, rhs,
                  pl.BlockSpec((None, SUBLANES, ct), lambda b, c, r: (b, 0, c)), halves],
        out_specs=halves,
        out_shape=jax.ShapeDtypeStruct((nb, 2, half, d_b), BF16),
        compiler_params=_params("parallel", "parallel", "arbitrary"),
        name="fourier_dft",
    )(tce, tse, tco, tso, cc, sc, see, dee, soo, doo, xh, sz4)
    return out.reshape(t, d_b)


def _merge_kernel(a_ref, f_ref, wa_ref, wb_ref, sa_ref, sb_ref, o_ref):
    for rows in _row_blocks(a_ref.shape[0]):
        ya = _dot(a_ref[rows, :], wa_ref[...])
        yb = _dot(f_ref[rows, :], wb_ref[...])
        m = sa_ref[rows, :].astype(F32) * ya + sb_ref[rows, :].astype(F32) * yb
        o_ref[rows, :] = m.astype(o_ref.dtype)


def _merge(a, fb, w_a, w_b, layer, sg, bm, bn):
    t, d_a = a.shape
    d_b = fb.shape[1]
    d = w_a.shape[2]
    nd = d // bn
    return pl.pallas_call(
        _merge_kernel,
        grid=(t // bm, nd),
        in_specs=[pl.BlockSpec((bm, d_a), lambda i, j: (i, 0)),
                  pl.BlockSpec((bm, d_b), lambda i, j: (i, 0)),
                  pl.BlockSpec((None, d_a, bn), lambda i, j: (layer, 0, j)),
                  pl.BlockSpec((None, d_b, bn), lambda i, j: (layer, 0, j)),
                  pl.BlockSpec((bm, bn), lambda i, j: (i, j)),
                  pl.BlockSpec((bm, bn), lambda i, j: (i, nd + j))],
        out_specs=pl.BlockSpec((bm, bn), lambda i, j: (i, j)),
        out_shape=jax.ShapeDtypeStruct((t, d), BF16),
        compiler_params=_params("parallel", "arbitrary"),
        name="branch_merge",
    )(a, fb, w_a, w_b, sg, sg)


def _out_kernel(m_ref, w_ref, x_ref, xo_ref, xb_ref, r_ref, ssq_ref, *, n_tiles, d):
    j = pl.program_id(1)
    for rows in _row_blocks(m_ref.shape[0]):
        xn = x_ref[rows, :] + _dot(m_ref[rows, :], w_ref[...])
        xo_ref[rows, :] = xn
        xb_ref[rows, :] = xn.astype(xb_ref.dtype)
        ssq_ref[j, rows, :] = _lane_fold(xn * xn)

    @pl.when(j == n_tiles - 1)
    def _():
        for rows in _row_blocks(m_ref.shape[0]):
            tot = sum(ssq_ref[t, rows, :] for t in range(n_tiles))
            ms = jnp.sum(tot, axis=-1, keepdims=True) / d
            r_ref[rows, :] = jnp.broadcast_to(lax.rsqrt(ms + EPS), tot.shape)


def _out_proj(m, w_out, layer, x, bm, bn):
    t, d = m.shape
    n_tiles = d // bn
    tile = pl.BlockSpec((bm, bn), lambda i, j: (i, j))
    return pl.pallas_call(
        functools.partial(_out_kernel, n_tiles=n_tiles, d=d),
        grid=(t // bm, n_tiles),
        in_specs=[pl.BlockSpec((bm, d), lambda i, j: (i, 0)),
                  pl.BlockSpec((None, d, bn), lambda i, j: (layer, 0, j)),
                  tile],
        out_specs=[tile, tile, pl.BlockSpec((bm, LANES), lambda i, j: (i, 0))],
        out_shape=[jax.ShapeDtypeStruct((t, d), F32), jax.ShapeDtypeStruct((t, d), BF16),
                   jax.ShapeDtypeStruct((t, LANES), F32)],
        scratch_shapes=[pltpu.VMEM((n_tiles, bm, LANES), F32)],
        compiler_params=_params("parallel", "arbitrary"),
        name="out_proj",
    )(m, w_out, x)


def _out_final_kernel(m_ref, w_ref, x_ref, g_ref, y_ref, *, bn, d):
    m = m_ref[...]
    ssq = jnp.zeros((m.shape[0], LANES), F32)
    col_tiles = [slice(c, c + bn) for c in range(0, d, bn)]
    for cols in col_tiles:
        xn = x_ref[:, cols] + _dot(m, w_ref[:, cols])
        y_ref[:, cols] = xn
        ssq = ssq + _lane_fold(xn * xn)
    scale = lax.rsqrt(jnp.sum(ssq, axis=-1, keepdims=True) / d + EPS)
    for cols in col_tiles:
        y_ref[:, cols] = y_ref[:, cols] * scale * g_ref[:, cols]


def _out_proj_final(m, w_out, layer, x, g, bm, bn):
    t, d = m.shape
    rows_full = pl.BlockSpec((bm, d), lambda i: (i, 0))
    vmem = d * d * 2 + 2 * bm * d * (2 + 4 + 4) + bm * d * 2 + 8 * bm * bn * 4
    return pl.pallas_call(
        functools.partial(_out_final_kernel, bn=bn, d=d),
        grid=(t // bm,),
        in_specs=[rows_full,
                  pl.BlockSpec((None, d, d), lambda i: (layer, 0, 0), pipeline_mode=pl.Buffered(1)),
                  rows_full,
                  pl.BlockSpec((1, d), lambda i: (0, 0))],
        out_specs=rows_full,
        out_shape=jax.ShapeDtypeStruct((t, d), F32),
        compiler_params=pltpu.CompilerParams(dimension_semantics=("parallel",), vmem_limit_bytes=vmem),
        name="out_proj_final_norm",
    )(m, w_out, x, g.reshape(1, d))


def _cos_sin(k, m, n, scale):
    ang = ((k * m) % n).astype(F32) * np.float32(2.0 * np.pi / n)
    return jnp.cos(ang) * np.float32(scale), jnp.sin(ang) * np.float32(-scale)


def _seq_tables(seq):
    half, quarter = seq // 2, seq // 4
    k = jnp.arange(half, dtype=jnp.int32)[:, None]
    m = jnp.arange(quarter, dtype=jnp.int32)[None, :]
    ce, se = _cos_sin(k, 2 * m, seq, seq ** -0.5)
    cphi, nsphi = _cos_sin(k, 1, seq, 1.0)
    co = ce * cphi - se * nsphi
    so = se * cphi + ce * nsphi
    i = jnp.arange(quarter, dtype=jnp.int32)
    ij = i[:, None] + i[None, :]
    mirror_even = ij == quarter
    mirror_odd = ij == quarter - 1
    return tuple(a.astype(BF16) for a in (ce, se, co, so, mirror_even, mirror_odd))


def _trunk(x3, weights, tables):
    (w_in, sgu_ln_g, sgu_ln_b, w_s, b_s_full, w_a, w_b, b_gate, w_out, final_g, cc, sc) = weights
    bsz, seq, d = x3.shape
    t = bsz * seq
    depth, d_a, d_b = w_a.shape[0], w_a.shape[1], w_b.shape[1]
    bm = _tile(t, ROW_TILE)
    bn = _tile(d_a, COL_TILE)
    bn2 = _tile(d_a, 2 * COL_TILE)
    half = seq // 2
    rt = _tile(half, ROW_TILE if half <= ROW_TILE else ROW_TILE // 2)
    x = x3.reshape(t, d)
    for l in range(depth):
        if l == 0:
            gv, mu, rs, h, rstd = _gelu_v_first(x, w_in, l, d_a, d_a, _tile(t, ROW_TILE // 2), bn2)
        else:
            gv, mu, rs = _gelu_v(h, rstd, w_in, l, d_a, d_a, bm, bn2)
        a = _gate_a(h, rstd, w_in, l, 0, 2 * d_a, gv, mu, rs, sgu_ln_g[l], sgu_ln_b[l], w_s, b_s_full, bm, bn)
        xe, xo, sz = _fourier_in(h, rstd, w_in, l, 3 * d_a, 3 * d_a + d_b, d_b, bm, bn)
        sg = _gates(h, rstd, w_in, l, 3 * d_a + 2 * d_b, b_gate[l].reshape(1, 2 * d), bm, bn2)
        fb = _fourier_mix(xe, xo, sz, tables, cc, sc, seq, rt, bn)
        m = _merge(a, fb, w_a, w_b, l, sg, bm, bn2)
        if l + 1 < depth:
            x, h, rstd = _out_proj(m, w_out, l, x, bm, bn)
        else:
            y = _out_proj_final(m, w_out, l, x, final_g, _tile(t, ROW_SUB), bn)
    return y.reshape(bsz, seq, d)


def kernel(x_prompt, x_sample, norm_g, w_in, sgu_ln_g, sgu_ln_b, w_spatial, b_spatial, w_a, w_b, b_gate, w_out, final_g):
    d_b = w_b.shape[1]
    gd = d_b // B_GROUPS
    hd = w_a.shape[1] // b_spatial.shape[1]
    i = jnp.arange(gd, dtype=jnp.int32)
    cc, nsc = _cos_sin(i[:, None], i[None, :], gd, gd ** -0.5)
    b_s_full = jnp.broadcast_to(b_spatial[..., None], b_spatial.shape + (hd,))
    w_in_g = (norm_g[:, :, None] * w_in).astype(BF16)
    weights = (w_in_g, sgu_ln_g, sgu_ln_b, w_spatial.astype(BF16), b_s_full,
               w_a.astype(BF16), w_b.astype(BF16), b_gate, w_out.astype(BF16), final_g,
               cc.astype(BF16), (-nsc).astype(BF16))
    return tuple(_trunk(x3, weights, _seq_tables(x3.shape[1])) for x3 in (x_prompt, x_sample))
```

```python
import functools

import numpy as np
import jax
import jax.numpy as jnp
from jax import lax
from jax.experimental import pallas as pl
from jax.experimental.pallas import tpu as pltpu

B_GROUPS = 8
EPS = 1e-6
VMEM_LIMIT_BYTES = 56 * 1024 * 1024
ROW_TILE = 1024
COL_TILE = 512
ROW_SUB = 256
LANES = 128
SUBLANES = 8
F32 = jnp.float32
BF16 = jnp.bfloat16


def _params(*sem):
    return pltpu.CompilerParams(dimension_semantics=sem, vmem_limit_bytes=VMEM_LIMIT_BYTES)


def _dot(a, b):
    return jnp.dot(a, b, preferred_element_type=F32)


def _row_blocks(n_rows):
    rs = ROW_SUB if n_rows % ROW_SUB == 0 else n_rows
    return [slice(r, r + rs) for r in range(0, n_rows, rs)]


def _tile(n, pref):
    return pref if n % pref == 0 else n


def _rowwide(r, n):
    reps = n // r.shape[1]
    return jnp.concatenate([r] * reps, axis=1) if reps > 1 else r


def _rowscale(p, r):
    return p * _rowwide(r, p.shape[1])


def _lane_fold(x):
    return sum(x[:, k:k + LANES] for k in range(0, x.shape[1], LANES))


def _gelu_v_kernel(h_ref, r_ref, w_ref, gv_ref, mu_ref, rs_ref, *, bn):
    _gelu_v_body(h_ref, r_ref, w_ref, gv_ref, mu_ref, rs_ref, bn)


def _gelu_v_first_kernel(x_ref, w_ref, gv_ref, mu_ref, rs_ref, h_ref, r_ref, *, bn):
    d = x_ref.shape[1]
    for rows in _row_blocks(x_ref.shape[0]):
        ssq = jnp.zeros((rows.stop - rows.start, LANES), F32)
        for c in range(0, d, bn):
            x = x_ref[rows, c:c + bn]
            h_ref[rows, c:c + bn] = x.astype(h_ref.dtype)
            ssq = ssq + _lane_fold(x * x)
        ms = jnp.sum(ssq, axis=-1, keepdims=True) / d
        r_ref[rows, :] = jnp.broadcast_to(lax.rsqrt(ms + EPS), ssq.shape)
    _gelu_v_body(h_ref, r_ref, w_ref, gv_ref, mu_ref, rs_ref, bn)


def _gelu_v_body(h_ref, r_ref, w_ref, gv_ref, mu_ref, rs_ref, bn):
    d_a = w_ref.shape[1]
    for rows in _row_blocks(h_ref.shape[0]):
        h, r = h_ref[rows, :], r_ref[rows, :]
        s1 = s2 = jnp.zeros((rows.stop - rows.start, LANES), F32)
        for c in range(0, d_a, bn):
            gv = jax.nn.gelu(_rowscale(_dot(h, w_ref[:, c:c + bn]), r))
            gv_ref[rows, c:c + bn] = gv
            s1 = s1 + _lane_fold(gv)
            s2 = s2 + _lane_fold(gv * gv)
        mu = jnp.sum(s1, axis=-1, keepdims=True) / d_a
        ex2 = jnp.sum(s2, axis=-1, keepdims=True) / d_a
        var = jnp.maximum(ex2 - mu * mu, 0.0)
        mu_ref[rows, :] = jnp.broadcast_to(mu, s1.shape)
        rs_ref[rows, :] = jnp.broadcast_to(lax.rsqrt(var + EPS), s1.shape)


def _gelu_v_specs(t, d, d_a, layer, col0, bm):
    stat = pl.BlockSpec((bm, LANES), lambda i: (i, 0))
    weight = pl.BlockSpec((None, d, d_a), lambda i: (layer, 0, col0 // d_a), pipeline_mode=pl.Buffered(1))
    rows_in = pl.BlockSpec((bm, d), lambda i: (i, 0))
    gv = pl.BlockSpec((bm, d_a), lambda i: (i, 0))
    stat_shape = jax.ShapeDtypeStruct((t, LANES), F32)
    return stat, weight, rows_in, gv, stat_shape


def _gelu_v(h, rstd, w_in, layer, col0, d_a, bm, bn):
    t, d = h.shape
    stat, weight, rows_in, gv, stat_shape = _gelu_v_specs(t, d, d_a, layer, col0, bm)
    return pl.pallas_call(
        functools.partial(_gelu_v_kernel, bn=bn),
        grid=(t // bm,),
        in_specs=[rows_in, stat, weight],
        out_specs=[gv, stat, stat],
        out_shape=[jax.ShapeDtypeStruct((t, d_a), F32), stat_shape, stat_shape],
        compiler_params=_params("parallel"),
        name="inproj_gelu_v",
    )(h, rstd, w_in)


def _gelu_v_first(x, w_in, layer, col0, d_a, bm, bn):
    t, d = x.shape
    stat, weight, rows_in, gv, stat_shape = _gelu_v_specs(t, d, d_a, layer, col0, bm)
    return pl.pallas_call(
        functools.partial(_gelu_v_first_kernel, bn=bn),
        grid=(t // bm,),
        in_specs=[rows_in, weight],
        out_specs=[gv, stat, stat, rows_in, stat],
        out_shape=[jax.ShapeDtypeStruct((t, d_a), F32), stat_shape, stat_shape,
                   jax.ShapeDtypeStruct((t, d), BF16), stat_shape],
        compiler_params=_params("parallel"),
        name="inproj_gelu_v_first",
    )(x, w_in)


def _gate_a_kernel(h_ref, r_ref, wu_ref, wz_ref, gv_ref, mu_ref, rs_ref, lng_ref, lnb_ref, ws_ref, bs_ref, o_ref, *,
                   chunk, hd):
    for rows in _row_blocks(h_ref.shape[0]):
        h, r = h_ref[rows, :], r_ref[rows, :]
        u = jax.nn.gelu(_rowscale(_dot(h, wu_ref[...]), r))
        z = jax.nn.silu(_rowscale(_dot(h, wz_ref[...]), r))
        vn = _rowscale(gv_ref[rows, :] - _rowwide(mu_ref[rows, :], u.shape[1]), rs_ref[rows, :])
        vn = (vn * lng_ref[...] + lnb_ref[...]).astype(BF16)
        n_chunks = vn.shape[0] // chunk
        per_head = []
        for hh in range(vn.shape[1] // hd):
            cols = slice(hh * hd, (hh + 1) * hd)
            rhs = jnp.concatenate([vn[c * chunk:(c + 1) * chunk, cols] for c in range(n_chunks)], axis=1)
            res = _dot(ws_ref[hh], rhs)
            per_head.append(jnp.concatenate(
                [res[:, c * hd:(c + 1) * hd] + bs_ref[hh] for c in range(n_chunks)], axis=0))
        mixed = jnp.concatenate(per_head, axis=1)
        o_ref[rows, :] = (u * mixed * z).astype(o_ref.dtype)


def _gate_a(h, rstd, w_in, layer, col_u, col_z, gv, mu, rs, ln_g, ln_b, w_s, b_s_full, bm, bn):
    t, d = h.shape
    d_a = gv.shape[1]
    _, heads, chunk, _ = w_s.shape
    hd = d_a // heads
    hpt = bn // hd
    assert all((b.stop - b.start) % chunk == 0 for b in _row_blocks(bm)), "row sub-blocks must hold whole chunks"
    bu, bz = col_u // bn, col_z // bn
    stat = pl.BlockSpec((bm, LANES), lambda i, j: (i, 0))
    tile = pl.BlockSpec((bm, bn), lambda i, j: (i, j))
    vec = pl.BlockSpec((1, bn), lambda i, j: (0, j))
    return pl.pallas_call(
        functools.partial(_gate_a_kernel, chunk=chunk, hd=hd),
        grid=(t // bm, d_a // bn),
        in_specs=[pl.BlockSpec((bm, d), lambda i, j: (i, 0)),
                  stat,
                  pl.BlockSpec((None, d, bn), lambda i, j: (layer, 0, bu + j)),
                  pl.BlockSpec((None, d, bn), lambda i, j: (layer, 0, bz + j)),
                  tile, stat, stat, vec, vec,
                  pl.BlockSpec((None, hpt, chunk, chunk), lambda i, j: (layer, j, 0, 0)),
                  pl.BlockSpec((None, hpt, chunk, hd), lambda i, j: (layer, j, 0, 0))],
        out_specs=tile,
        out_shape=jax.ShapeDtypeStruct((t, d_a), BF16),
        compiler_params=_params("parallel", "arbitrary"),
        name="inproj_gate_a",
    )(h, rstd, w_in, w_in, gv, mu, rs, ln_g.reshape(1, d_a), ln_b.reshape(1, d_a), w_s, b_s_full)


def _fourier_in_kernel(h_ref, r_ref, wx_ref, wz_ref, xe_ref, xo_ref, sz_ref, x_scr):
    for rows in _row_blocks(h_ref.shape[0]):
        h, r = h_ref[rows, :], r_ref[rows, :]
        xb = _rowscale(_dot(h, wx_ref[...]), r)
        n_half = xb.shape[0] // 2
        half_rows = slice(rows.start // 2, rows.start // 2 + n_half)
        for q in range(xb.shape[1] // LANES):
            cols = slice(q * LANES, (q + 1) * LANES)
            x_scr[q, rows, :] = xb[:, cols]
            xe_ref[half_rows, cols] = x_scr[q, pl.ds(rows.start, n_half, stride=2), :].astype(xe_ref.dtype)
            xo_ref[half_rows, cols] = x_scr[q, pl.ds(rows.start + 1, n_half, stride=2), :].astype(xo_ref.dtype)
        sz_ref[rows, :] = jax.nn.silu(_rowscale(_dot(h, wz_ref[...]), r)).astype(sz_ref.dtype)


def _fourier_in(h, rstd, w_in, layer, col_x, col_z, d_b, bm, bn):
    t, d = h.shape
    bx, bz = col_x // bn, col_z // bn
    half = jax.ShapeDtypeStruct((t // 2, d_b), BF16)
    half_tile = pl.BlockSpec((bm // 2, bn), lambda i, j: (i, j))
    return pl.pallas_call(
        _fourier_in_kernel,
        grid=(t // bm, d_b // bn),
        in_specs=[pl.BlockSpec((bm, d), lambda i, j: (i, 0)),
                  pl.BlockSpec((bm, LANES), lambda i, j: (i, 0)),
                  pl.BlockSpec((None, d, bn), lambda i, j: (layer, 0, bx + j)),
                  pl.BlockSpec((None, d, bn), lambda i, j: (layer, 0, bz + j))],
        out_specs=[half_tile, half_tile, pl.BlockSpec((bm, bn), lambda i, j: (i, j))],
        out_shape=[half, half, jax.ShapeDtypeStruct((t, d_b), BF16)],
        scratch_shapes=[pltpu.VMEM((bn // LANES, bm, LANES), F32)],
        compiler_params=_params("parallel", "arbitrary"),
        name="inproj_fourier",
    )(h, rstd, w_in, w_in)


def _sig_kernel(h_ref, r_ref, w_ref, b_ref, o_ref):
    for rows in _row_blocks(h_ref.shape[0]):
        g = _rowscale(_dot(h_ref[rows, :], w_ref[...]), r_ref[rows, :]) + b_ref[...]
        o_ref[rows, :] = jax.nn.sigmoid(g).astype(o_ref.dtype)


def _gates(h, rstd, w_in, layer, col0, b_gate_flat, bm, bn):
    t, d = h.shape
    n = b_gate_flat.shape[1]
    blk0 = col0 // bn
    return pl.pallas_call(
        _sig_kernel,
        grid=(t // bm, n // bn),
        in_specs=[pl.BlockSpec((bm, d), lambda i, j: (i, 0)),
                  pl.BlockSpec((bm, LANES), lambda i, j: (i, 0)),
                  pl.BlockSpec((None, d, bn), lambda i, j: (layer, 0, blk0 + j)),
                  pl.BlockSpec((1, bn), lambda i, j: (0, j))],
        out_specs=pl.BlockSpec((bm, bn), lambda i, j: (i, j)),
        out_shape=jax.ShapeDtypeStruct((t, n), BF16),
        compiler_params=_params("parallel", "arbitrary"),
        name="inproj_gates",
    )(h, rstd, w_in, b_gate_flat)


def _fold_kernel(pe_ref, po_ref, xe_ref, xo_ref, see_ref, dee_ref, soo_ref, doo_ref, xh_ref):
    q = pe_ref.shape[0]
    for x_ref, p_ref, s_ref, d_ref in ((xe_ref, pe_ref, see_ref, dee_ref), (xo_ref, po_ref, soo_ref, doo_ref)):
        for rows in _row_blocks(q):
            lo = x_ref[rows, :].astype(F32)
            mirror = _dot(p_ref[rows, :], x_ref[q:, :])
            s_ref[rows, :] = (lo + mirror).astype(s_ref.dtype)
            d_ref[rows, :] = (lo - mirror).astype(d_ref.dtype)
    xh_ref[...] = jnp.broadcast_to(xe_ref[q:q + 1, :].astype(F32), xh_ref.shape)


def _fnet_kernel(tce_ref, tse_ref, tco_ref, tso_ref, cc_ref, sc_ref, pe_ref, po_ref, xe_ref, xo_ref, sz_ref, o_ref,
                 see_ref, dee_ref, soo_ref, doo_ref, xh_ref, *, gd, scale):
    @pl.when(pl.program_id(2) == 0)
    def _():
        _fold_kernel(pe_ref, po_ref, xe_ref, xo_ref, see_ref, dee_ref, soo_ref, doo_ref, xh_ref)

    rt, ct = tce_ref.shape[0], o_ref.shape[-1]
    k_base = pl.program_id(2) * rt
    for rows in _row_blocks(rt):
        k0 = k_base + rows.start + lax.broadcasted_iota(jnp.int32, (rows.stop - rows.start, ct), 0)
        nyquist = jnp.where((k0 & 1) == 0, scale, -scale) * xh_ref[0:1, :]
        er = _dot(tce_ref[rows, :], see_ref[...]) + nyquist
        ei = _dot(tse_ref[rows, :], dee_ref[...])
        orr = _dot(tco_ref[rows, :], soo_ref[...])
        oi = _dot(tso_ref[rows, :], doo_ref[...])
        for half, (gr, gi) in enumerate(((er + orr, ei + oi), (er - orr, ei - oi))):
            gr, gi = gr.astype(BF16), gi.astype(BF16)
            for g in range(ct // gd):
                cols = slice(g * gd, (g + 1) * gd)
                y = _dot(gr[:, cols], cc_ref[...]) + _dot(gi[:, cols], sc_ref[...])
                o_ref[half, rows, cols] = (y * sz_ref[half, rows, cols].astype(F32)).astype(o_ref.dtype)


def _fourier_mix(xe, xo, sz, tables, cc, sc, seq, rt, ct):
    t, d_b = sz.shape
    nb, half, quarter = t // seq, seq // 2, seq // 4
    tce, tse, tco, tso, pe, po = tables
    gd = cc.shape[0]
    sz4 = sz.reshape(nb, 2, half, d_b)
    tab = pl.BlockSpec((rt, quarter), lambda b, c, r: (r, 0))
    chan = pl.BlockSpec((gd, gd), lambda b, c, r: (0, 0))
    perm = pl.BlockSpec((quarter, quarter), lambda b, c, r: (0, 0))
    parity = pl.BlockSpec((half, ct), lambda b, c, r: (b, c))
    halves = pl.BlockSpec((None, 2, rt, ct), lambda b, c, r: (b, 0, r, c))
    out = pl.pallas_call(
        functools.partial(_fnet_kernel, gd=gd, scale=float(seq) ** -0.5),
        grid=(nb, d_b // ct, half // rt),
        in_specs=[tab, tab, tab, tab, chan, chan, perm, perm, parity, parity, halves],
        out_specs=halves,
        out_shape=jax.ShapeDtypeStruct((nb, 2, half, d_b), BF16),
        scratch_shapes=[pltpu.VMEM((quarter, ct), BF16)] * 4 + [pltpu.VMEM((SUBLANES, ct), F32)],
        compiler_params=_params("parallel", "parallel", "arbitrary"),
        name="fourier_mix",
    )(tce, tse, tco, tso, cc, sc, pe, po, xe, xo, sz4)
    return out.reshape(t, d_b)


def _merge_kernel(a_ref, f_ref, wa_ref, wb_ref, sa_ref, sb_ref, o_ref):
    for rows in _row_blocks(a_ref.shape[0]):
        ya = _dot(a_ref[rows, :], wa_ref[...])
        yb = _dot(f_ref[rows, :], wb_ref[...])
        m = sa_ref[rows, :].astype(F32) * ya + sb_ref[rows, :].astype(F32) * yb
        o_ref[rows, :] = m.astype(o_ref.dtype)


def _merge(a, fb, w_a, w_b, layer, sg, bm, bn):
    t, d_a = a.shape
    d_b = fb.shape[1]
    d = w_a.shape[2]
    nd = d // bn
    return pl.pallas_call(
        _merge_kernel,
        grid=(t // bm, nd),
        in_specs=[pl.BlockSpec((bm, d_a), lambda i, j: (i, 0)),
                  pl.BlockSpec((bm, d_b), lambda i, j: (i, 0)),
                  pl.BlockSpec((None, d_a, bn), lambda i, j: (layer, 0, j)),
                  pl.BlockSpec((None, d_b, bn), lambda i, j: (layer, 0, j)),
                  pl.BlockSpec((bm, bn), lambda i, j: (i, j)),
                  pl.BlockSpec((bm, bn), lambda i, j: (i, nd + j))],
        out_specs=pl.BlockSpec((bm, bn), lambda i, j: (i, j)),
        out_shape=jax.ShapeDtypeStruct((t, d), BF16),
        compiler_params=_params("parallel", "arbitrary"),
        name="branch_merge",
    )(a, fb, w_a, w_b, sg, sg)


def _out_kernel(m_ref, w_ref, x_ref, xo_ref, xb_ref, r_ref, ssq_ref, *, n_tiles, d):
    j = pl.program_id(1)
    for rows in _row_blocks(m_ref.shape[0]):
        xn = x_ref[rows, :] + _dot(m_ref[rows, :], w_ref[...])
        xo_ref[rows, :] = xn
        xb_ref[rows, :] = xn.astype(xb_ref.dtype)
        ssq_ref[j, rows, :] = _lane_fold(xn * xn)

    @pl.when(j == n_tiles - 1)
    def _():
        for rows in _row_blocks(m_ref.shape[0]):
            tot = sum(ssq_ref[t, rows, :] for t in range(n_tiles))
            ms = jnp.sum(tot, axis=-1, keepdims=True) / d
            r_ref[rows, :] = jnp.broadcast_to(lax.rsqrt(ms + EPS), tot.shape)


def _out_proj(m, w_out, layer, x, bm, bn):
    t, d = m.shape
    n_tiles = d // bn
    tile = pl.BlockSpec((bm, bn), lambda i, j: (i, j))
    return pl.pallas_call(
        functools.partial(_out_kernel, n_tiles=n_tiles, d=d),
        grid=(t // bm, n_tiles),
        in_specs=[pl.BlockSpec((bm, d), lambda i, j: (i, 0)),
                  pl.BlockSpec((None, d, bn), lambda i, j: (layer, 0, j)),
                  tile],
        out_specs=[tile, tile, pl.BlockSpec((bm, LANES), lambda i, j: (i, 0))],
        out_shape=[jax.ShapeDtypeStruct((t, d), F32), jax.ShapeDtypeStruct((t, d), BF16),
                   jax.ShapeDtypeStruct((t, LANES), F32)],
        scratch_shapes=[pltpu.VMEM((n_tiles, bm, LANES), F32)],
        compiler_params=_params("parallel", "arbitrary"),
        name="out_proj",
    )(m, w_out, x)


def _out_final_kernel(m_ref, w_ref, x_ref, g_ref, y_ref, *, bn, d):
    m = m_ref[...]
    ssq = jnp.zeros((m.shape[0], LANES), F32)
    col_tiles = [slice(c, c + bn) for c in range(0, d, bn)]
    for cols in col_tiles:
        xn = x_ref[:, cols] + _dot(m, w_ref[:, cols])
        y_ref[:, cols] = xn
        ssq = ssq + _lane_fold(xn * xn)
    scale = lax.rsqrt(jnp.sum(ssq, axis=-1, keepdims=True) / d + EPS)
    for cols in col_tiles:
        y_ref[:, cols] = y_ref[:, cols] * scale * g_ref[:, cols]


def _out_proj_final(m, w_out, layer, x, g, bm, bn):
    t, d = m.shape
    rows_full = pl.BlockSpec((bm, d), lambda i: (i, 0))
    vmem = d * d * 2 + 2 * bm * d * (2 + 4 + 4) + bm * d * 2 + 8 * bm * bn * 4
    return pl.pallas_call(
        functools.partial(_out_final_kernel, bn=bn, d=d),
        grid=(t // bm,),
        in_specs=[rows_full,
                  pl.BlockSpec((None, d, d), lambda i: (layer, 0, 0), pipeline_mode=pl.Buffered(1)),
                  rows_full,
                  pl.BlockSpec((1, d), lambda i: (0, 0))],
        out_specs=rows_full,
        out_shape=jax.ShapeDtypeStruct((t, d), F32),
        compiler_params=pltpu.CompilerParams(dimension_semantics=("parallel",), vmem_limit_bytes=vmem),
        name="out_proj_final_norm",
    )(m, w_out, x, g.reshape(1, d))


def _cos_sin(k, m, n, scale):
    ang = ((k * m) % n).astype(F32) * np.float32(2.0 * np.pi / n)
    return jnp.cos(ang) * np.float32(scale), jnp.sin(ang) * np.float32(-scale)


def _seq_tables(seq):
    half, quarter = seq // 2, seq // 4
    k = jnp.arange(half, dtype=jnp.int32)[:, None]
    m = jnp.arange(quarter, dtype=jnp.int32)[None, :]
    ce, se = _cos_sin(k, 2 * m, seq, seq ** -0.5)
    cphi, nsphi = _cos_sin(k, 1, seq, 1.0)
    co = ce * cphi - se * nsphi
    so = se * cphi + ce * nsphi
    i = jnp.arange(quarter, dtype=jnp.int32)
    ij = i[:, None] + i[None, :]
    mirror_even = ij == quarter
    mirror_odd = ij == quarter - 1
    return tuple(a.astype(BF16) for a in (ce, se, co, so, mirror_even, mirror_odd))


def _trunk(x3, weights, tables):
    (w_in, sgu_ln_g, sgu_ln_b, w_s, b_s_full, w_a, w_b, b_gate, w_out, final_g, cc, sc) = weights
    bsz, seq, d = x3.shape
    t = bsz * seq
    depth, d_a, d_b = w_a.shape[0], w_a.shape[1], w_b.shape[1]
    bm = _tile(t, ROW_TILE)
    bn = _tile(d_a, COL_TILE)
    bn2 = _tile(d_a, 2 * COL_TILE)
    half = seq // 2
    rt = _tile(half, ROW_TILE if half <= ROW_TILE else ROW_TILE // 2)
    x = x3.reshape(t, d)
    for l in range(depth):
        if l == 0:
            gv, mu, rs, h, rstd = _gelu_v_first(x, w_in, l, d_a, d_a, _tile(t, ROW_TILE // 2), bn2)
        else:
            gv, mu, rs = _gelu_v(h, rstd, w_in, l, d_a, d_a, bm, bn2)
        a = _gate_a(h, rstd, w_in, l, 0, 2 * d_a, gv, mu, rs, sgu_ln_g[l], sgu_ln_b[l], w_s, b_s_full, bm, bn)
        xe, xo, sz = _fourier_in(h, rstd, w_in, l, 3 * d_a, 3 * d_a + d_b, d_b, bm, bn)
        sg = _gates(h, rstd, w_in, l, 3 * d_a + 2 * d_b, b_gate[l].reshape(1, 2 * d), bm, bn2)
        fb = _fourier_mix(xe, xo, sz, tables, cc, sc, seq, rt, bn)
        m = _merge(a, fb, w_a, w_b, l, sg, bm, bn2)
        if l + 1 < depth:
            x, h, rstd = _out_proj(m, w_out, l, x, bm, bn)
        else:
            y = _out_proj_final(m, w_out, l, x, final_g, _tile(t, ROW_SUB), bn)
    return y.reshape(bsz, seq, d)


def kernel(x_prompt, x_sample, norm_g, w_in, sgu_ln_g, sgu_ln_b, w_spatial, b_spatial, w_a, w_b, b_gate, w_out, final_g):
    d_b = w_b.shape[1]
    gd = d_b // B_GROUPS
    hd = w_a.shape[1] // b_spatial.shape[1]
    i = jnp.arange(gd, dtype=jnp.int32)
    cc, nsc = _cos_sin(i[:, None], i[None, :], gd, gd ** -0.5)
    b_s_full = jnp.broadcast_to(b_spatial[..., None], b_spatial.shape + (hd,))
    w_in_g = (norm_g[:, :, None] * w_in).astype(BF16)
    weights = (w_in_g, sgu_ln_g, sgu_ln_b, w_spatial.astype(BF16), b_s_full,
               w_a.astype(BF16), w_b.astype(BF16), b_gate, w_out.astype(BF16), final_g,
               cc.astype(BF16), (-nsc).astype(BF16))
    return tuple(_trunk(x3, weights, _seq_tables(x3.shape[1])) for x3 in (x_prompt, x_sample))
```
